```python
import math
import jax
import jax.numpy as jnp
from jax import lax
import numpy as np

D_MODEL = 1024
BATCH = 2
SEQ = 16384
DEPTH = 1
DEC_BATCH = 128
DEC_SEQ = 4
PAST_LEN = 8192
PAGE_SIZE = 128

SB_HEADS = 8
SB_HEAD_DIM = 64
DIFF_HEADS = 4
DIFF_HEAD_DIM = 64
MEM_HEADS = 4
MEM_HEAD_DIM = 128
N_MEM = 256
N_BRANCH = 3
SB_WIDTH = SB_HEADS * SB_HEAD_DIM
DIFF_QK_WIDTH = DIFF_HEADS * 2 * DIFF_HEAD_DIM
DIFF_V_WIDTH = DIFF_HEADS * 2 * DIFF_HEAD_DIM
MEM_WIDTH = MEM_HEADS * MEM_HEAD_DIM
BRANCH_WIDTH = 512
GATE_WIDTH = N_BRANCH * D_MODEL
IN_SIZES = (SB_WIDTH, SB_WIDTH, SB_WIDTH, DIFF_QK_WIDTH, DIFF_QK_WIDTH, DIFF_V_WIDTH, MEM_WIDTH, GATE_WIDTH)
IN_WIDTH = sum(IN_SIZES)
D_FF = 2816
CONV_WIDTH = 3
Q_BLOCK = 128
ROPE_THETA = 10000.0
EPS = 1e-6

kernel_name = 'gated_parallel_stickbreak_diffattn_memory_convglu_step'


def rms_norm(x, g):
    xf = x.astype(jnp.float32)
    y = xf * lax.rsqrt(jnp.mean(xf * xf, axis=-1, keepdims=True) + EPS)
    return (y * g.astype(jnp.float32)).astype(x.dtype)


def rotary(x, pos):
    half = x.shape[-1] // 2
    inv_freq = jnp.power(ROPE_THETA, -jnp.arange(half, dtype=jnp.float32) / half)
    ang = pos.astype(jnp.float32)[:, None] * inv_freq[None, :]
    shape = (1, pos.shape[0]) + (1,) * (x.ndim - 3) + (half,)
    cos = jnp.cos(ang).reshape(shape)
    sin = jnp.sin(ang).reshape(shape)
    x1 = x[..., :half].astype(jnp.float32)
    x2 = x[..., half:].astype(jnp.float32)
    return jnp.concatenate([x1 * cos - x2 * sin, x2 * cos + x1 * sin], axis=-1).astype(x.dtype)


def stick_breaking_core(q, k, v, q_pos, k_pos):
    z = jnp.einsum('bqhd,bkhd->bhqk', q, k, preferred_element_type=jnp.float32) * (SB_HEAD_DIM ** -0.5)
    visible = k_pos[None, :] < q_pos[:, None]
    log_keep = jnp.where(visible, jax.nn.log_sigmoid(-z), 0.0)
    log_between = lax.cumsum(log_keep, axis=3, reverse=True) - log_keep
    weight = jnp.where(visible, jnp.exp(jax.nn.log_sigmoid(z) + log_between), 0.0)
    return jnp.einsum('bhqk,bkhd->bqhd', weight.astype(v.dtype), v, preferred_element_type=jnp.float32).astype(v.dtype)


def diff_core(q, k, v, q_pos, k_pos, lam):
    s = jnp.einsum('bqhmd,bkhmd->bhmqk', q, k, preferred_element_type=jnp.float32) * (DIFF_HEAD_DIM ** -0.5)
    causal = k_pos[None, :] <= q_pos[:, None]
    probs = jax.nn.softmax(jnp.where(causal, s, -jnp.inf), axis=-1)
    w = probs[:, :, 0] - lam * probs[:, :, 1]
    return jnp.einsum('bhqk,bkhe->bqhe', w.astype(v.dtype), v, preferred_element_type=jnp.float32).astype(v.dtype)


def over_query_blocks(core, q, k, v, k_pos):
    B, T = q.shape[:2]
    nb = T // Q_BLOCK
    qb = jnp.moveaxis(q.reshape((B, nb, Q_BLOCK) + q.shape[2:]), 1, 0)
    starts = jnp.arange(nb) * Q_BLOCK

    def one_block(args):
        q_blk, start = args
        return core(q_blk, k, v, start + jnp.arange(Q_BLOCK), k_pos)

    ob = lax.map(one_block, (qb, starts))
    return jnp.moveaxis(ob, 0, 1).reshape((B, T) + ob.shape[3:])


def memory_kv(mem, p):
    B, M, _ = mem.shape
    m_k, m_v = jnp.split(rms_norm(mem, p['norm_mem']) @ p['w_mem_kv'], 2, axis=-1)
    shape = (B, M, MEM_HEADS, MEM_HEAD_DIM)
    return rms_norm(m_k.reshape(shape), p['mem_k_norm']), m_v.reshape(shape)


def memory_attend(q, k, v):
    s = jnp.einsum('bthd,bmhd->bhtm', q, k, preferred_element_type=jnp.float32) * (MEM_HEAD_DIM ** -0.5)
    probs = jax.nn.softmax(s, axis=-1)
    return jnp.einsum('bhtm,bmhd->bthd', probs.astype(v.dtype), v, preferred_element_type=jnp.float32).astype(v.dtype)


def mixer_inputs(h, pos, p):
    B, T, _ = h.shape
    split_at = np.cumsum(IN_SIZES)[:-1].tolist()
    sb_q, sb_k, sb_v, d_q, d_k, d_v, m_q, gate = jnp.split(h @ p['w_in'], split_at, axis=-1)
    sb_shape = (B, T, SB_HEADS, SB_HEAD_DIM)
    dqk_shape = (B, T, DIFF_HEADS, 2, DIFF_HEAD_DIM)
    d_q = rotary(rms_norm(d_q.reshape(dqk_shape), p['diff_q_norm']), pos)
    d_k = rotary(rms_norm(d_k.reshape(dqk_shape), p['diff_k_norm']), pos)
    d_v = d_v.reshape(B, T, DIFF_HEADS, 2 * DIFF_HEAD_DIM)
    m_q = rms_norm(m_q.reshape(B, T, MEM_HEADS, MEM_HEAD_DIM), p['mem_q_norm'])
    return (sb_q.reshape(sb_shape), sb_k.reshape(sb_shape), sb_v.reshape(sb_shape), d_q, d_k, d_v, m_q, gate)


def merge_branches(o_sb, o_diff, o_mem, gate, p, lam_init):
    B, T = o_sb.shape[:2]
    o_diff = rms_norm(o_diff, p['diff_subln']) * (1.0 - lam_init)
    o = jnp.stack([o_sb.reshape(B, T, BRANCH_WIDTH), o_diff.reshape(B, T, BRANCH_WIDTH), o_mem.reshape(B, T, BRANCH_WIDTH)], axis=2)
    br = jnp.einsum('btnc,ncd->btnd', o, p['w_branch'])
    g = jax.nn.sigmoid(gate.reshape(B, T, N_BRANCH, D_MODEL) + p['b_gate'])
    return jnp.sum(g * br, axis=2) @ p['w_out']


def conv_ffn(x, prefix, p):
    T = x.shape[1]
    a, u = jnp.split(rms_norm(x, p['norm_ffn']) @ p['w_ffn_up'], 2, axis=-1)
    a_ext = jnp.concatenate([prefix.astype(a.dtype), a], axis=1)
    w = p['conv_w']
    a_conv = p['conv_b']
    for i in range(CONV_WIDTH):
        a_conv = a_conv + w[i] * a_ext[:, i:i + T]
    y = (jax.nn.silu(a_conv) * u) @ p['w_ffn_down']
    return y, a_ext[:, T:]


def gather_pages(pool, page_table):
    g = jnp.take(pool, page_table, axis=0)
    return g.reshape((g.shape[0], g.shape[1] * g.shape[2]) + g.shape[3:])


def prompt_layer(x, mem, p, lam, lam_init):
    B, T, _ = x.shape
    pos = jnp.arange(T)
    sb_q, sb_k, sb_v, d_q, d_k, d_v, m_q, gate = mixer_inputs(rms_norm(x, p['norm_mix']), pos, p)
    m_k, m_v = memory_kv(mem, p)
    o_sb = over_query_blocks(stick_breaking_core, sb_q, sb_k, sb_v, pos)
    o_diff = over_query_blocks(lambda q, k, v, qp, kp: diff_core(q, k, v, qp, kp, lam), d_q, d_k, d_v, pos)
    o_mem = memory_attend(m_q, m_k, m_v)
    x = x + merge_branches(o_sb, o_diff, o_mem, gate, p, lam_init)
    f, conv_state = conv_ffn(x, jnp.zeros((B, CONV_WIDTH - 1, D_FF), x.dtype), p)
    return x + f, (sb_k, sb_v, d_k, d_v, m_k, m_v, conv_state)


def sample_layer(x, pool_sb_k, pool_sb_v, pool_d_k, pool_d_v, mem_k, mem_v, conv_prev, page_table, past_len, p, lam, lam_init):
    T = x.shape[1]
    pos = past_len + jnp.arange(T)
    k_pos = jnp.arange(past_len + T)
    sb_q, sb_k, sb_v, d_q, d_k, d_v, m_q, gate = mixer_inputs(rms_norm(x, p['norm_mix']), pos, p)

    def with_past(pool, new):
        return jnp.concatenate([gather_pages(pool, page_table).astype(new.dtype), new], axis=1)

    o_sb = stick_breaking_core(sb_q, with_past(pool_sb_k, sb_k), with_past(pool_sb_v, sb_v), pos, k_pos)
    o_diff = diff_core(d_q, with_past(pool_d_k, d_k), with_past(pool_d_v, d_v), pos, k_pos, lam)
    o_mem = memory_attend(m_q, mem_k, mem_v)
    x = x + merge_branches(o_sb, o_diff, o_mem, gate, p, lam_init)
    f, conv_state = conv_ffn(x, conv_prev, p)
    return x + f, (sb_k, sb_v, d_k, d_v, conv_state)


def setup_inputs(seed: int = 0) -> dict:
    key = jax.random.key(seed)
    ks = jax.random.split(key, 40)
    f32 = jnp.float32
    n_pages = PAST_LEN // PAGE_SIZE
    n_used = DEC_BATCH * n_pages
    n_pool = (5 * n_used + 3) // 4

    def nrm(k, shape, scale=1.0):
        return jax.random.normal(k, shape, f32) * scale

    def gain(k, shape):
        return 1.0 + 0.01 * jax.random.normal(k, shape, f32)

    page_table = jax.random.permutation(ks[0], n_pool)[:n_used].reshape(DEC_BATCH, n_pages).astype(jnp.int32)
    return {
        'x_prompt': nrm(ks[1], (BATCH, SEQ, D_MODEL)),
        'x_sample': nrm(ks[2], (DEC_BATCH, DEC_SEQ, D_MODEL)),
        'cache_sb_k': nrm(ks[3], (DEPTH, n_pool, PAGE_SIZE, SB_HEADS, SB_HEAD_DIM)),
        'cache_sb_v': nrm(ks[4], (DEPTH, n_pool, PAGE_SIZE, SB_HEADS, SB_HEAD_DIM)),
        'cache_diff_k': nrm(ks[5], (DEPTH, n_pool, PAGE_SIZE, DIFF_HEADS, 2, DIFF_HEAD_DIM)),
        'cache_diff_v': nrm(ks[6], (DEPTH, n_pool, PAGE_SIZE, DIFF_HEADS, 2 * DIFF_HEAD_DIM)),
        'cache_mem_k': nrm(ks[7], (DEPTH, DEC_BATCH, N_MEM, MEM_HEADS, MEM_HEAD_DIM)),
        'cache_mem_v': nrm(ks[8], (DEPTH, DEC_BATCH, N_MEM, MEM_HEADS, MEM_HEAD_DIM)),
        'state_conv': nrm(ks[9], (DEPTH, DEC_BATCH, CONV_WIDTH - 1, D_FF)),
        'page_table': page_table,
        'mem_prompt': nrm(ks[10], (BATCH, N_MEM, D_MODEL)),
        'norm_mix': gain(ks[11], (DEPTH, D_MODEL)),
        'norm_mem': gain(ks[12], (DEPTH, D_MODEL)),
        'w_in': nrm(ks[13], (DEPTH, D_MODEL, IN_WIDTH), D_MODEL ** -0.5),
        'b_gate': nrm(ks[14], (DEPTH, N_BRANCH, D_MODEL), 0.01),
        'diff_q_norm': gain(ks[15], (DEPTH, DIFF_HEAD_DIM)),
        'diff_k_norm': gain(ks[16], (DEPTH, DIFF_HEAD_DIM)),
        'lambda_q1': nrm(ks[17], (DEPTH, DIFF_HEAD_DIM), 0.1),
        'lambda_k1': nrm(ks[18], (DEPTH, DIFF_HEAD_DIM), 0.1),
        'lambda_q2': nrm(ks[19], (DEPTH, DIFF_HEAD_DIM), 0.1),
        'lambda_k2': nrm(ks[20], (DEPTH, DIFF_HEAD_DIM), 0.1),
        'diff_subln': gain(ks[21], (DEPTH, 2 * DIFF_HEAD_DIM)),
        'w_mem_kv': nrm(ks[22], (DEPTH, D_MODEL, 2 * MEM_WIDTH), D_MODEL ** -0.5),
        'mem_q_norm': gain(ks[23], (DEPTH, MEM_HEAD_DIM)),
        'mem_k_norm': gain(ks[24], (DEPTH, MEM_HEAD_DIM)),
        'w_branch': nrm(ks[25], (DEPTH, N_BRANCH, BRANCH_WIDTH, D_MODEL), BRANCH_WIDTH ** -0.5),
        'w_out': nrm(ks[26], (DEPTH, D_MODEL, D_MODEL), D_MODEL ** -0.5),
        'norm_ffn': gain(ks[27], (DEPTH, D_MODEL)),
        'w_ffn_up': nrm(ks[28], (DEPTH, D_MODEL, 2 * D_FF), D_MODEL ** -0.5),
        'conv_w': nrm(ks[29], (DEPTH, CONV_WIDTH, D_FF), CONV_WIDTH ** -0.5),
        'conv_b': nrm(ks[30], (DEPTH, D_FF), 0.01),
        'w_ffn_down': nrm(ks[31], (DEPTH, D_FF, D_MODEL), D_FF ** -0.5),
    }


def reference(x_prompt, x_sample, cache_sb_k, cache_sb_v, cache_diff_k, cache_diff_v, cache_mem_k, cache_mem_v, state_conv, page_table, mem_prompt, norm_mix, norm_mem, w_in, b_gate, diff_q_norm, diff_k_norm, lambda_q1, lambda_k1, lambda_q2, lambda_k2, diff_subln, w_mem_kv, mem_q_norm, mem_k_norm, w_branch, w_out, norm_ffn, w_ffn_up, conv_w, conv_b, w_ffn_down):
    past_len = page_table.shape[1] * PAGE_SIZE
    xp, xs = x_prompt, x_sample
    prompt_states, sample_states = [], []
    for l in range(DEPTH):
        p = {
            'norm_mix': norm_mix[l], 'norm_mem': norm_mem[l], 'w_in': w_in[l], 'b_gate': b_gate[l],
            'diff_q_norm': diff_q_norm[l], 'diff_k_norm': diff_k_norm[l], 'diff_subln': diff_subln[l],
            'w_mem_kv': w_mem_kv[l], 'mem_q_norm': mem_q_norm[l], 'mem_k_norm': mem_k_norm[l],
            'w_branch': w_branch[l], 'w_out': w_out[l], 'norm_ffn': norm_ffn[l],
            'w_ffn_up': w_ffn_up[l], 'conv_w': conv_w[l], 'conv_b': conv_b[l], 'w_ffn_down': w_ffn_down[l],
        }
        lam_init = 0.8 - 0.6 * math.exp(-0.3 * l)
        lam = (jnp.exp(jnp.sum(lambda_q1[l].astype(jnp.float32) * lambda_k1[l].astype(jnp.float32)))
               - jnp.exp(jnp.sum(lambda_q2[l].astype(jnp.float32) * lambda_k2[l].astype(jnp.float32))) + lam_init)
        xp, st_p = prompt_layer(xp, mem_prompt, p, lam, lam_init)
        xs, st_s = sample_layer(xs, cache_sb_k[l], cache_sb_v[l], cache_diff_k[l], cache_diff_v[l], cache_mem_k[l], cache_mem_v[l], state_conv[l], page_table, past_len, p, lam, lam_init)
        prompt_states.append(st_p)
        sample_states.append(st_s)
    sbk_p, sbv_p, dk_p, dv_p, mk_p, mv_p, cv_p = [jnp.stack(s) for s in zip(*prompt_states)]
    sbk_s, sbv_s, dk_s, dv_s, cv_s = [jnp.stack(s) for s in zip(*sample_states)]
    return (xp, xs, sbk_p, sbv_p, dk_p, dv_p, mk_p, mv_p, cv_p, sbk_s, sbv_s, dk_s, dv_s, cv_s)
```

```python
import functools
import math

import jax
import jax.numpy as jnp
from jax import lax
from jax.experimental import pallas as pl
from jax.experimental.pallas import tpu as pltpu

F32 = jnp.float32
BF16 = jnp.bfloat16

D_MODEL = 1024
SB_HEADS = 8
DIFF_HEADS = 4
MEM_HEADS = 4
HEAD64 = 64
HEAD128 = 128
BRANCH_WIDTH = 512
N_BRANCH = 3
PAGE_SIZE = 128
CONV_WIDTH = 3
ROPE_THETA = 10000.0
EPS = 1e-6
N_ATTN_SECTIONS = 7

LANES = 128
SUBLANES = 8
VMEM_LIMIT = 56 * 1024 * 1024

NEG_BIG = -1e30
SB_DONE_LOG = -110.0

_NT = (((1,), (1,)), ((), ()))


def _dot(a, b):
    return jnp.dot(a, b, preferred_element_type=F32)


def _dot_nt(a, b):
    return lax.dot_general(a, b, _NT, preferred_element_type=F32)


def _split_dot(x, m):
    hi = x.astype(BF16)
    lo = (x - hi.astype(F32)).astype(BF16)
    return _dot(hi, m) + _dot(lo, m)


def _rms_rows(x, g):
    ms = jnp.mean(x * x, axis=-1, keepdims=True)
    return x * lax.rsqrt(ms + EPS) * g


def _seg_norm(y, seg, gain):
    ms = _split_dot(y * y, seg)
    return y * lax.rsqrt(ms + EPS) * gain


def _tile_lanes(x, width):
    reps = width // x.shape[-1]
    return x if reps == 1 else jnp.concatenate([x] * reps, axis=-1)


def _cparams(sem, limit=VMEM_LIMIT):
    return pltpu.CompilerParams(dimension_semantics=sem, vmem_limit_bytes=limit)


def _const_spec(shape):
    nd = len(shape)
    return pl.BlockSpec(shape, lambda *_: (0,) * nd, pipeline_mode=pl.Buffered(1))


def _in_proj_kernel(x_ref, g_ref, w_ref, cos_ref, sin_ref, seg64_ref, seg128_ref,
                    dqn_ref, dkn_ref, mqn_ref,
                    sbq_b, sbk_f, sbk_b, sbv_f, sbv_b, dq_b, dk_f, dk_b, dv_f, dv_b, mq_b):
    h = _rms_rows(x_ref[...], g_ref[...]).astype(BF16)
    width = BRANCH_WIDTH

    def proj(s):
        return _dot(h, w_ref[:, s * width:(s + 1) * width])

    cos = _tile_lanes(cos_ref[...], width)
    sin = _tile_lanes(sin_ref[...], width)
    lane = lax.broadcasted_iota(jnp.int32, cos.shape, 1)
    low_half = (lane % HEAD64) < (HEAD64 // 2)

    def rope(y):
        swapped = jnp.where(low_half, pltpu.roll(y, width - HEAD64 // 2, 1),
                            pltpu.roll(y, HEAD64 // 2, 1))
        return y * cos + swapped * sin

    scale64 = HEAD64 ** -0.5

    y = proj(0)
    sbq_b[...] = (y * scale64).astype(BF16)
    y = proj(1)
    sbk_f[...] = y
    sbk_b[...] = y.astype(BF16)
    y = proj(2)
    sbv_f[...] = y
    sbv_b[...] = y.astype(BF16)
    y = rope(_seg_norm(proj(3), seg64_ref[...], dqn_ref[...]))
    dq_b[...] = (y * scale64).astype(BF16)
    y = rope(_seg_norm(proj(4), seg64_ref[...], dkn_ref[...]))
    dk_f[...] = y
    dk_b[...] = y.astype(BF16)
    y = proj(5)
    dv_f[...] = y
    dv_b[...] = y.astype(BF16)
    y = _seg_norm(proj(6), seg128_ref[...], mqn_ref[...])
    mq_b[...] = y.astype(BF16)


def _in_proj(x2d, g, w_attn, cos_tab, sin_tab, seg64, seg128, dqn, dkn, mqn, tm):
    m = x2d.shape[0]
    n_tab = cos_tab.shape[0] // tm
    row = lambda i: (i, 0)
    tab = lambda i: (i % n_tab, 0)
    wd = BRANCH_WIDTH
    out_dtypes = (BF16, F32, BF16, F32, BF16, BF16, F32, BF16, F32, BF16, BF16)
    return pl.pallas_call(
        _in_proj_kernel,
        out_shape=tuple(jax.ShapeDtypeStruct((m, wd), dt) for dt in out_dtypes),
        grid=(m // tm,),
        in_specs=[
            pl.BlockSpec((tm, D_MODEL), row),
            _const_spec((1, D_MODEL)),
            _const_spec(w_attn.shape),
            pl.BlockSpec((tm, LANES), tab),
            pl.BlockSpec((tm, LANES), tab),
            _const_spec((wd, wd)),
            _const_spec((wd, wd)),
            _const_spec((1, wd)),
            _const_spec((1, wd)),
            _const_spec((1, wd)),
        ],
        out_specs=tuple(pl.BlockSpec((tm, wd), row) for _ in out_dtypes),
        compiler_params=_cparams(("parallel",)),
        name="in_proj",
    )(x2d, g, w_attn, cos_tab, sin_tab, seg64, seg128, dqn, dkn, mqn)


def _mem_kv_kernel(x_ref, g_ref, w_ref, seg128_ref, kn_ref, mk_f, mk_b, mv_f, mv_b):
    h = _rms_rows(x_ref[...], g_ref[...]).astype(BF16)
    wd = BRANCH_WIDTH
    k = _seg_norm(_dot(h, w_ref[:, :wd]), seg128_ref[...], kn_ref[...])
    mk_f[...] = k
    mk_b[...] = k.astype(BF16)
    v = _dot(h, w_ref[:, wd:])
    mv_f[...] = v
    mv_b[...] = v.astype(BF16)


def _mem_kv(mem2d, g, w_kv, seg128, kn):
    m = mem2d.shape[0]
    wd = BRANCH_WIDTH
    full = lambda shape: pl.BlockSpec(shape, lambda i: (0, 0))
    return pl.pallas_call(
        _mem_kv_kernel,
        out_shape=tuple(jax.ShapeDtypeStruct((m, wd), dt) for dt in (F32, BF16, F32, BF16)),
        grid=(1,),
        in_specs=[full(mem2d.shape), full(g.shape), full(w_kv.shape), full(seg128.shape),
                  full(kn.shape)],
        out_specs=tuple(full((m, wd)) for _ in range(4)),
        compiler_params=_cparams(("arbitrary",)),
        name="mem_kv",
    )(mem2d, g, w_kv, seg128, kn)


def _sb_tile(z, c, uext, vis):
    tk = z.shape[-1]
    sp = jnp.log1p(jnp.exp(-jnp.abs(z)))
    log_keep = -(jnp.maximum(z, 0.0) + sp)
    log_take = jnp.minimum(z, 0.0) - sp
    if vis is not None:
        log_keep = jnp.where(vis, log_keep, 0.0)
    ee = _split_dot(log_keep, uext)
    w = jnp.exp(log_take + ee[:, :tk] + _tile_lanes(c, tk))
    if vis is not None:
        w = jnp.where(vis, w, 0.0)
    return w, ee[:, tk:]


def _sb_prompt_kernel(q_ref, k_ref, v_ref, u_ref, o_ref, acc_ref, c_ref, *, tq, tk):
    q0 = pl.program_id(2) * tq
    q = q_ref[0]
    lane = lax.broadcasted_iota(jnp.int32, (tq, LANES), 1)
    first = lane < HEAD64
    zero = jnp.zeros_like(q)
    q_heads = (jnp.where(first, q, zero), jnp.where(first, zero, q))
    acc_ref[...] = jnp.zeros_like(acc_ref)
    c_ref[...] = jnp.zeros_like(c_ref)

    def tile(kstart, masked):
        k = k_ref[0, pl.ds(kstart, tk), :]
        v = v_ref[0, pl.ds(kstart, tk), :]
        vis = None
        if masked:
            qpos = q0 + lax.broadcasted_iota(jnp.int32, (tq, tk), 0)
            kpos = kstart + lax.broadcasted_iota(jnp.int32, (tq, tk), 1)
            vis = kpos < qpos
        pv = []
        for hh in range(2):
            z = _dot_nt(q_heads[hh], k)
            c = c_ref[hh]
            w, tot = _sb_tile(z, c, u_ref[...], vis)
            pv.append(_dot(w.astype(BF16), v))
            c_ref[hh] = c + tot
        acc_ref[...] += jnp.where(first, pv[0], pv[1])

    for d in reversed(range(tq // tk)):
        tile(pl.multiple_of(q0 + d * tk, tk), True)

    def cond(state):
        kt, done = state
        return jnp.logical_and(kt >= 0, done == 0)

    def body(state):
        kt, _ = state
        tile(pl.multiple_of(kt * tk, tk), False)
        done = (jnp.max(c_ref[...]) < SB_DONE_LOG).astype(jnp.int32)
        return kt - 1, done

    lax.while_loop(cond, body, (q0 // tk - 1, jnp.int32(0)))
    o_ref[0] = acc_ref[...].astype(o_ref.dtype)


def _reverse_cumsum_matrix(tk):
    j = jnp.arange(tk)
    u = (j[:, None] > j[None, :]).astype(BF16)
    return jnp.concatenate([u, jnp.ones((tk, LANES), BF16)], axis=1)


def _sb_prompt(q, k, v, tq, tk):
    b, t, wd = q.shape
    n_pairs = wd // LANES
    kernel = functools.partial(_sb_prompt_kernel, tq=tq, tk=tk)
    qspec = pl.BlockSpec((1, tq, LANES), lambda bi, hp, qi: (bi, qi, hp))
    kvspec = pl.BlockSpec((1, t, LANES), lambda bi, hp, qi: (bi, 0, hp))
    return pl.pallas_call(
        kernel,
        out_shape=jax.ShapeDtypeStruct((b, t, wd), BF16),
        grid=(b, n_pairs, t // tq),
        in_specs=[qspec, kvspec, kvspec, _const_spec((tk, tk + LANES))],
        out_specs=qspec,
        scratch_shapes=[pltpu.VMEM((tq, LANES), F32), pltpu.VMEM((2, tq, LANES), F32)],
        compiler_params=_cparams(("parallel", "parallel", "parallel")),
        name="sb_prompt",
    )(q, k, v, _reverse_cumsum_matrix(tk))


def _lambda_value(lam_ref, lam_init):
    lv = lam_ref[...]
    a = jnp.sum(lv[0:1] * lv[1:2], axis=-1, keepdims=True)
    b = jnp.sum(lv[2:3] * lv[3:4], axis=-1, keepdims=True)
    return jnp.exp(a) - jnp.exp(b) + lam_init


def _online_softmax_step(s, v, m_ref, l_ref, acc_ref, idx):
    m_prev = m_ref[idx]
    m_new = jnp.maximum(m_prev, jnp.max(s, axis=-1, keepdims=True))
    p = jnp.exp(s - m_new)
    alpha = jnp.exp(m_prev - m_new)
    l_ref[idx] = alpha * l_ref[idx] + jnp.sum(p, axis=-1, keepdims=True)
    acc_ref[idx] = alpha * acc_ref[idx] + _dot(p.astype(BF16), v)
    m_ref[idx] = m_new


def _diff_prompt_kernel(lam_ref, sub_ref, q_ref, k_ref, v_ref, o_ref, m_ref, l_ref, acc_ref,
                        *, tq, lam_init):
    qi = pl.program_id(2)
    q = q_ref[0]
    lane = lax.broadcasted_iota(jnp.int32, (tq, LANES), 1)
    first = lane < HEAD64
    zero = jnp.zeros_like(q)
    q_maps = (jnp.where(first, q, zero), jnp.where(first, zero, q))
    m_ref[...] = jnp.full_like(m_ref, NEG_BIG)
    l_ref[...] = jnp.zeros_like(l_ref)
    acc_ref[...] = jnp.zeros_like(acc_ref)

    def tile(kstart, masked):
        k = k_ref[0, pl.ds(kstart, tq), :]
        v = v_ref[0, pl.ds(kstart, tq), :]
        for mm in range(2):
            s = _dot_nt(q_maps[mm], k)
            if masked:
                row = lax.broadcasted_iota(jnp.int32, (tq, tq), 0)
                col = lax.broadcasted_iota(jnp.int32, (tq, tq), 1)
                s = jnp.where(col <= row, s, NEG_BIG)
            _online_softmax_step(s, v, m_ref, l_ref, acc_ref, mm)

    tile(pl.multiple_of(qi * tq, tq), True)

    def body(kt, carry):
        tile(pl.multiple_of(kt * tq, tq), False)
        return carry

    lax.fori_loop(0, qi, body, 0)

    lam = _lambda_value(lam_ref, lam_init)
    o = acc_ref[0] / l_ref[0] - lam * (acc_ref[1] / l_ref[1])
    o = _rms_rows(o, sub_ref[...]) * (1.0 - lam_init)
    o_ref[0] = o.astype(o_ref.dtype)


def _diff_prompt(lam_vecs, subln, q, k, v, tq, lam_init):
    b, t, wd = q.shape
    n_heads = wd // LANES
    kernel = functools.partial(_diff_prompt_kernel, tq=tq, lam_init=lam_init)
    qspec = pl.BlockSpec((1, tq, LANES), lambda bi, h, qi: (bi, qi, h))
    kvspec = pl.BlockSpec((1, t, LANES), lambda bi, h, qi: (bi, 0, h))
    return pl.pallas_call(
        kernel,
        out_shape=jax.ShapeDtypeStruct((b, t, wd), BF16),
        grid=(b, n_heads, t // tq),
        in_specs=[_const_spec(lam_vecs.shape), _const_spec(subln.shape), qspec, kvspec, kvspec],
        out_specs=qspec,
        scratch_shapes=[pltpu.VMEM((2, tq, 1), F32), pltpu.VMEM((2, tq, 1), F32),
                        pltpu.VMEM((2, tq, LANES), F32)],
        compiler_params=_cparams(("parallel", "parallel", "parallel")),
        name="diff_prompt",
    )(lam_vecs, subln, q, k, v)


def _mem_attend_kernel(q_ref, k_ref, v_ref, o_ref):
    q = q_ref[0]
    k = k_ref[0].astype(BF16)
    v = v_ref[0].astype(BF16)
    scale = HEAD128 ** -0.5
    outs = []
    for h in range(MEM_HEADS):
        sl = slice(h * HEAD128, (h + 1) * HEAD128)
        s = _dot_nt(q[:, sl], k[:, sl]) * scale
        p = jnp.exp(s - jnp.max(s, axis=-1, keepdims=True))
        denom = jnp.sum(p, axis=-1, keepdims=True)
        outs.append(_dot(p.astype(BF16), v[:, sl]) / denom)
    o_ref[0] = jnp.concatenate(outs, axis=-1).astype(o_ref.dtype)


def _mem_attend(q, k, v, tq, kv_per_row_block):
    g, r, wd = q.shape
    n_mem = k.shape[1]
    qspec = pl.BlockSpec((1, tq, wd), lambda gi, ri: (gi, ri, 0))
    kvspec = pl.BlockSpec((1, n_mem, wd), lambda gi, ri: (gi, 0, 0))
    return pl.pallas_call(
        _mem_attend_kernel,
        out_shape=jax.ShapeDtypeStruct((g, r, wd), BF16),
        grid=(g, r // tq),
        in_specs=[qspec, kvspec, kvspec],
        out_specs=qspec,
        compiler_params=_cparams(("parallel", "parallel")),
        name="mem_attend_" + kv_per_row_block,
    )(q, k, v)


def _sb_sample_kernel(pt_ref, q_ref, newk_ref, newv_ref, k_ref, v_ref, u_ref, gmask_ref, o_ref,
                      acc_ref, c_ref, *, n_new):
    j = pl.program_id(1)
    q = q_ref[0]
    rows = q.shape[0]

    def page(k, v, vis):
        z = _dot_nt(q, k.astype(BF16))
        c = c_ref[...]
        w, tot = _sb_tile(z, c, u_ref[...], vis)
        acc_ref[...] += _dot(w.astype(BF16), v.astype(BF16))
        c_ref[...] = c + tot

    @pl.when(j == 0)
    def _():
        acc_ref[...] = jnp.zeros_like(acc_ref)
        c_ref[...] = jnp.zeros_like(c_ref)
        t = lax.broadcasted_iota(jnp.int32, (rows, PAGE_SIZE), 0) // SUBLANES
        s = lax.broadcasted_iota(jnp.int32, (rows, PAGE_SIZE), 1)
        page(newk_ref[0], newv_ref[0], s < t)

    page(k_ref[0], v_ref[0], None)

    @pl.when(j == pl.num_programs(1) - 1)
    def _():
        masked = acc_ref[...] * gmask_ref[...]
        o = jnp.sum(masked.reshape(n_new, SUBLANES, masked.shape[-1]), axis=1)
        o_ref[0] = o.astype(o_ref.dtype)


def _diff_sample_kernel(pt_ref, lam_ref, sub_ref, q_ref, newk_ref, newv_ref, k_ref, v_ref,
                        gmask_ref, o_ref, m_ref, l_ref, acc_ref, *, n_new, lam_init):
    j = pl.program_id(1)
    q = q_ref[0]
    rows = q.shape[0]

    def page(k, v, vis):
        s = _dot_nt(q, k.astype(BF16))
        if vis is not None:
            s = jnp.where(vis, s, NEG_BIG)
        m_prev = m_ref[...]
        m_new = jnp.maximum(m_prev, jnp.max(s, axis=-1, keepdims=True))
        p = jnp.exp(s - m_new)
        alpha = jnp.exp(m_prev - m_new)
        l_ref[...] = alpha * l_ref[...] + jnp.sum(p, axis=-1, keepdims=True)
        acc_ref[...] = alpha * acc_ref[...] + _dot(p.astype(BF16), v.astype(BF16))
        m_ref[...] = m_new

    @pl.when(j == 0)
    def _():
        m_ref[...] = jnp.full_like(m_ref, NEG_BIG)
        l_ref[...] = jnp.zeros_like(l_ref)
        acc_ref[...] = jnp.zeros_like(acc_ref)
        t = lax.broadcasted_iota(jnp.int32, (rows, PAGE_SIZE), 0) // SUBLANES
        s = lax.broadcasted_iota(jnp.int32, (rows, PAGE_SIZE), 1)
        page(newk_ref[0], newv_ref[0], s <= t)

    page(k_ref[0], v_ref[0], None)

    @pl.when(j == pl.num_programs(1) - 1)
    def _():
        lam = _lambda_value(lam_ref, lam_init)
        r = lax.broadcasted_iota(jnp.int32, (rows, 1), 0)
        coef = jnp.where(r % 2 == 0, 1.0, -lam)
        contrib = acc_ref[...] * (coef / l_ref[...]) * gmask_ref[...]
        o = jnp.sum(contrib.reshape(n_new, SUBLANES, contrib.shape[-1]), axis=1)
        sub = sub_ref[...]
        outs = []
        for h in range(DIFF_HEADS):
            sl = slice(h * HEAD128, (h + 1) * HEAD128)
            outs.append(_rms_rows(o[:, sl], sub) * (1.0 - lam_init))
        o_ref[0] = jnp.concatenate(outs, axis=-1).astype(o_ref.dtype)


def _paged_specs(n_pages, wd, rows):
    per_b = lambda shape: pl.BlockSpec(shape, lambda b, j, pt: (b, 0, 0))
    page = pl.BlockSpec((1, PAGE_SIZE, wd), lambda b, j, pt: (pt[b, n_pages - 1 - j], 0, 0))
    const = lambda shape: pl.BlockSpec(shape, lambda b, j, pt: (0,) * len(shape))
    return per_b, page, const


def _sb_sample(page_table, qbd, newk, newv, pool_k, pool_v, gmask, n_new):
    db, rows, wd = qbd.shape
    n_pages = page_table.shape[1]
    per_b, page, const = _paged_specs(n_pages, wd, rows)
    uext = _reverse_cumsum_matrix(PAGE_SIZE)
    grid_spec = pltpu.PrefetchScalarGridSpec(
        num_scalar_prefetch=1,
        grid=(db, n_pages),
        in_specs=[per_b((1, rows, wd)), per_b((1, PAGE_SIZE, wd)), per_b((1, PAGE_SIZE, wd)),
                  page, page, const(uext.shape), const(gmask.shape)],
        out_specs=per_b((1, n_new, wd)),
        scratch_shapes=[pltpu.VMEM((rows, wd), F32), pltpu.VMEM((rows, LANES), F32)],
    )
    return pl.pallas_call(
        functools.partial(_sb_sample_kernel, n_new=n_new),
        out_shape=jax.ShapeDtypeStruct((db, n_new, wd), F32),
        grid_spec=grid_spec,
        compiler_params=_cparams(("parallel", "arbitrary")),
        name="sb_sample",
    )(page_table, qbd, newk, newv, pool_k, pool_v, uext, gmask)


def _diff_sample(page_table, lam_vecs, subln, qbd, newk, newv, pool_k, pool_v, gmask, n_new,
                 lam_init):
    db, rows, wd = qbd.shape
    n_pages = page_table.shape[1]
    per_b, page, const = _paged_specs(n_pages, wd, rows)
    grid_spec = pltpu.PrefetchScalarGridSpec(
        num_scalar_prefetch=1,
        grid=(db, n_pages),
        in_specs=[const(lam_vecs.shape), const(subln.shape),
                  per_b((1, rows, wd)), per_b((1, PAGE_SIZE, wd)), per_b((1, PAGE_SIZE, wd)),
                  page, page, const(gmask.shape)],
        out_specs=per_b((1, n_new, wd)),
        scratch_shapes=[pltpu.VMEM((rows, 1), F32), pltpu.VMEM((rows, 1), F32),
                        pltpu.VMEM((rows, wd), F32)],
    )
    return pl.pallas_call(
        functools.partial(_diff_sample_kernel, n_new=n_new, lam_init=lam_init),
        out_shape=jax.ShapeDtypeStruct((db, n_new, wd), F32),
        grid_spec=grid_spec,
        compiler_params=_cparams(("parallel", "arbitrary")),
        name="diff_sample",
    )(page_table, lam_vecs, subln, qbd, newk, newv, pool_k, pool_v, gmask)


def _merge_kernel(x_ref, osb_ref, odf_ref, omem_ref, g_ref, wg_ref, bg_ref, wb_ref, wo_ref,
                  o_ref):
    x = x_ref[...]
    h = _rms_rows(x, g_ref[...]).astype(BF16)
    merged = None
    for b, ob_ref in enumerate((osb_ref, odf_ref, omem_ref)):
        gate = _dot(h, wg_ref[:, b * D_MODEL:(b + 1) * D_MODEL]) + bg_ref[b:b + 1, :]
        term = jax.nn.sigmoid(gate) * _dot(ob_ref[...].astype(BF16), wb_ref[b])
        merged = term if merged is None else merged + term
    o_ref[...] = x + _dot(merged.astype(BF16), wo_ref[...])


def _merge(x2d, o_sb, o_diff, o_mem, g, w_gate, b_gate, w_branch, w_out, tm):
    m = x2d.shape[0]
    row = lambda i: (i, 0)
    wide = pl.BlockSpec((tm, D_MODEL), row)
    narrow = pl.BlockSpec((tm, BRANCH_WIDTH), row)
    return pl.pallas_call(
        _merge_kernel,
        out_shape=jax.ShapeDtypeStruct((m, D_MODEL), F32),
        grid=(m // tm,),
        in_specs=[wide, narrow, narrow, narrow, _const_spec(g.shape), _const_spec(w_gate.shape),
                  _const_spec(b_gate.shape), _const_spec(w_branch.shape),
                  _const_spec(w_out.shape)],
        out_specs=wide,
        compiler_params=_cparams(("parallel",)),
        name="merge",
    )(x2d, o_sb, o_diff, o_mem, g, w_gate, b_gate, w_branch, w_out)


def _ffn_chunks(d_ff):
    chunk = 256 if d_ff % 256 == 0 else LANES
    return chunk, d_ff // chunk


def _ffn_prompt_kernel(x_ref, g_ref, wup_ref, cw_ref, cb_ref, wdn_ref, o_ref, tail_ref, abuf_ref,
                       *, tm, d_ff):
    pad = SUBLANES

    @pl.when(pl.program_id(1) == 0)
    def _():
        abuf_ref[0:pad, :] = jnp.zeros((pad, d_ff), F32)

    x = x_ref[0]
    h = _rms_rows(x, g_ref[...]).astype(BF16)
    chunk, n_chunks = _ffn_chunks(d_ff)
    acc = x
    for c in range(n_chunks):
        sl = slice(c * chunk, (c + 1) * chunk)
        abuf_ref[pad:pad + tm, sl] = _dot(h, wup_ref[:, sl])
        conv = (cb_ref[:, sl]
                + cw_ref[0:1, sl] * abuf_ref[pad - 2:pad - 2 + tm, sl]
                + cw_ref[1:2, sl] * abuf_ref[pad - 1:pad - 1 + tm, sl]
                + cw_ref[2:3, sl] * abuf_ref[pad:pad + tm, sl])
        up = _dot(h, wup_ref[:, d_ff + c * chunk:d_ff + (c + 1) * chunk])
        acc = acc + _dot((jax.nn.silu(conv) * up).astype(BF16), wdn_ref[sl, :])
    o_ref[0] = acc
    last = abuf_ref[tm:tm + pad, :]
    tail_ref[0] = last
    abuf_ref[0:pad, :] = last


def _ffn_prompt(x3d, g, w_up, conv_w, conv_b, w_down, tm):
    b, t, d = x3d.shape
    d_ff = w_down.shape[0]
    kernel = functools.partial(_ffn_prompt_kernel, tm=tm, d_ff=d_ff)
    xspec = pl.BlockSpec((1, tm, d), lambda bi, ti: (bi, ti, 0))
    return pl.pallas_call(
        kernel,
        out_shape=(jax.ShapeDtypeStruct((b, t, d), F32),
                   jax.ShapeDtypeStruct((b, SUBLANES, d_ff), F32)),
        grid=(b, t // tm),
        in_specs=[xspec, _const_spec(g.shape), _const_spec(w_up.shape), _const_spec(conv_w.shape),
                  _const_spec(conv_b.shape), _const_spec(w_down.shape)],
        out_specs=(xspec, pl.BlockSpec((1, SUBLANES, d_ff), lambda bi, ti: (bi, 0, 0))),
        scratch_shapes=[pltpu.VMEM((tm + SUBLANES, d_ff), F32)],
        compiler_params=_cparams(("parallel", "arbitrary")),
        name="ffn_prompt",
    )(x3d, g, w_up, conv_w, conv_b, w_down)


def _ffn_sample_kernel(x_ref, g_ref, pre_ref, wa_ref, wu_ref, cw_ref, cb_ref, wdn_ref,
                       o_ref, a_ref, h_ref, *, n_new, db):
    c = pl.program_id(0)

    @pl.when(c == 0)
    def _():
        x = x_ref[...]
        h_ref[...] = _rms_rows(x, g_ref[...]).astype(BF16)
        o_ref[...] = x

    h = h_ref[...]
    a = _dot(h, wa_ref[...])
    a_ref[...] = a
    up = _dot(h, wu_ref[...])
    slabs = [pre_ref[0], pre_ref[1]] + [a[t * db:(t + 1) * db] for t in range(n_new)]
    conv = jnp.concatenate(
        [cb_ref[...] + cw_ref[0:1] * slabs[t] + cw_ref[1:2] * slabs[t + 1] + cw_ref[2:3] * slabs[t + 2]
         for t in range(n_new)], axis=0)
    o_ref[...] += _dot((jax.nn.silu(conv) * up).astype(BF16), wdn_ref[...])


def _ffn_sample(x_tm, g, prefix_tm, w_up, conv_w, conv_b, w_down, n_new, db):
    m, d = x_tm.shape
    d_ff = w_down.shape[0]
    chunk, n_chunks = _ffn_chunks(d_ff)
    kernel = functools.partial(_ffn_sample_kernel, n_new=n_new, db=db)
    fixed = lambda shape: pl.BlockSpec(shape, lambda c: (0,) * len(shape))
    return pl.pallas_call(
        kernel,
        out_shape=(jax.ShapeDtypeStruct((m, d), F32), jax.ShapeDtypeStruct((m, d_ff), F32)),
        grid=(n_chunks,),
        in_specs=[fixed((m, d)), fixed(g.shape),
                  pl.BlockSpec((CONV_WIDTH - 1, db, chunk), lambda c: (0, 0, c)),
                  pl.BlockSpec((d, chunk), lambda c: (0, c)),
                  pl.BlockSpec((d, chunk), lambda c: (0, n_chunks + c)),
                  pl.BlockSpec((CONV_WIDTH, chunk), lambda c: (0, c)),
                  pl.BlockSpec((1, chunk), lambda c: (0, c)),
                  pl.BlockSpec((chunk, d), lambda c: (c, 0))],
        out_specs=(fixed((m, d)), pl.BlockSpec((m, chunk), lambda c: (0, c))),
        scratch_shapes=[pltpu.VMEM((m, d), BF16)],
        compiler_params=_cparams(("arbitrary",)),
        name="ffn_sample",
    )(x_tm, g, prefix_tm, w_up, w_up, conv_w, conv_b, w_down)


def _rope_tables(pos):
    half = HEAD64 // 2
    inv_freq = jnp.power(ROPE_THETA, -jnp.arange(half, dtype=F32) / half)
    ang = pos.astype(F32)[:, None] * inv_freq[None, :]
    cos, sin = jnp.cos(ang), jnp.sin(ang)
    cos = jnp.concatenate([cos, cos], axis=-1)
    sin = jnp.concatenate([-sin, sin], axis=-1)
    reps = LANES // HEAD64
    return jnp.tile(cos, (1, reps)), jnp.tile(sin, (1, reps))


def _seg_matrix(width, seg):
    i = jnp.arange(width) // seg
    return ((i[:, None] == i[None, :]).astype(F32) / seg).astype(BF16)


def _tile_gain(gain, width):
    return jnp.tile(gain.astype(F32), width // gain.shape[0]).reshape(1, width)


def _block_diag_queries(q, gmask):
    db, n, wd = q.shape
    return (q[:, :, None, :] * gmask[None, None].astype(q.dtype)).reshape(db, n * SUBLANES, wd)


def _as_page(new, n_new):
    db, _, wd = new.shape
    return jnp.pad(new, ((0, 0), (0, PAGE_SIZE - n_new), (0, 0)))


def kernel(x_prompt, x_sample, cache_sb_k, cache_sb_v, cache_diff_k, cache_diff_v, cache_mem_k, cache_mem_v, state_conv, page_table, mem_prompt, norm_mix, norm_mem, w_in, b_gate, diff_q_norm, diff_k_norm, lambda_q1, lambda_k1, lambda_q2, lambda_k2, diff_subln, w_mem_kv, mem_q_norm, mem_k_norm, w_branch, w_out, norm_ffn, w_ffn_up, conv_w, conv_b, w_ffn_down):
    depth = w_in.shape[0]
    assert depth == 1, "single-layer step"
    l = 0
    lam_init = 0.8 - 0.6 * math.exp(-0.3 * l)
    bsz, seq, d = x_prompt.shape
    db, n_new, _ = x_sample.shape
    n_pages = page_table.shape[1]
    past_len = n_pages * PAGE_SIZE
    n_mem = mem_prompt.shape[1]
    wd = BRANCH_WIDTH
    d_ff = w_ffn_down.shape[1]
    n_attn = N_ATTN_SECTIONS * wd

    w_in_b = w_in[l].astype(BF16)
    w_attn, w_gate = w_in_b[:, :n_attn], w_in_b[:, n_attn:]
    w_branch_b = w_branch[l].astype(BF16)
    w_out_b = w_out[l].astype(BF16)
    w_up_b = w_ffn_up[l].astype(BF16)
    w_down_b = w_ffn_down[l].astype(BF16)
    w_kv_b = w_mem_kv[l].astype(BF16)
    g_mix = norm_mix[l].reshape(1, d)
    g_mem = norm_mem[l].reshape(1, d)
    g_ffn = norm_ffn[l].reshape(1, d)
    seg64, seg128 = _seg_matrix(wd, HEAD64), _seg_matrix(wd, HEAD128)
    dqn, dkn = _tile_gain(diff_q_norm[l], wd), _tile_gain(diff_k_norm[l], wd)
    mqn, mkn = _tile_gain(mem_q_norm[l], wd), _tile_gain(mem_k_norm[l], wd)
    subln = diff_subln[l].reshape(1, HEAD128)
    lam_vecs = jnp.stack([lambda_q1[l], lambda_k1[l], lambda_q2[l], lambda_k2[l]]).astype(F32)
    cw, cb = conv_w[l], conv_b[l].reshape(1, d_ff)
    group = jnp.arange(wd) // HEAD64
    gmask64 = (group[None, :] == jnp.arange(SUBLANES)[:, None]).astype(F32)
    gmask_sb = jnp.tile(gmask64, (n_new, 1))
    gmask_diff = jnp.tile(((group // 2)[None, :] == (jnp.arange(SUBLANES) // 2)[:, None]).astype(F32),
                          (n_new, 1))

    tm = min(512, seq)
    tq = min(256, seq)
    cos_p, sin_p = _rope_tables(jnp.arange(seq))
    (sbq_b, sbk_f, sbk_b, sbv_f, sbv_b, dq_b, dk_f, dk_b, dv_f, dv_b, mq_b) = _in_proj(
        x_prompt.reshape(bsz * seq, d), g_mix, w_attn, cos_p, sin_p, seg64, seg128, dqn, dkn, mqn, tm)
    mk_f, mk_b, mv_f, mv_b = _mem_kv(mem_prompt.reshape(bsz * n_mem, d), g_mem, w_kv_b, seg128, mkn)

    b3 = lambda a: a.reshape(bsz, seq, wd)
    o_sb = _sb_prompt(b3(sbq_b), b3(sbk_b), b3(sbv_b), tq, min(128, seq))
    o_diff = _diff_prompt(lam_vecs, subln, b3(dq_b), b3(dk_b), b3(dv_b), tq, lam_init)
    o_mem = _mem_attend(b3(mq_b), mk_b.reshape(bsz, n_mem, wd), mv_b.reshape(bsz, n_mem, wd),
                        tm, "prompt")
    x1 = _merge(x_prompt.reshape(bsz * seq, d), o_sb.reshape(-1, wd), o_diff.reshape(-1, wd),
                o_mem.reshape(-1, wd), g_mix, w_gate, b_gate[l], w_branch_b, w_out_b, tm)
    y_prompt, conv_tail = _ffn_prompt(x1.reshape(bsz, seq, d), g_ffn, w_up_b, cw, cb, w_down_b, tm)

    ms = db * n_new
    cos_s, sin_s = _rope_tables(past_len + jnp.arange(n_new))
    cos_s, sin_s = jnp.tile(cos_s, (db, 1)), jnp.tile(sin_s, (db, 1))
    (ssbq_b, ssbk_f, _, ssbv_f, _, sdq_b, sdk_f, _, sdv_f, _, smq_b) = _in_proj(
        x_sample.reshape(ms, d), g_mix, w_attn, cos_s, sin_s, seg64, seg128, dqn, dkn, mqn, ms)

    s3 = lambda a: a.reshape(db, n_new, wd)
    pool = lambda c: c[l].reshape(c.shape[1], PAGE_SIZE, wd)
    so_sb = _sb_sample(page_table, _block_diag_queries(s3(ssbq_b), gmask64),
                       _as_page(s3(ssbk_f), n_new), _as_page(s3(ssbv_f), n_new),
                       pool(cache_sb_k), pool(cache_sb_v), gmask_sb, n_new)
    so_diff = _diff_sample(page_table, lam_vecs, subln, _block_diag_queries(s3(sdq_b), gmask64),
                           _as_page(s3(sdk_f), n_new), _as_page(s3(sdv_f), n_new),
                           pool(cache_diff_k), pool(cache_diff_v), gmask_diff, n_new, lam_init)
    q_rows = 2 * SUBLANES
    q_pad = jnp.pad(s3(smq_b), ((0, 0), (0, q_rows - n_new), (0, 0)))
    so_mem = _mem_attend(q_pad, cache_mem_k[l].reshape(db, n_mem, wd),
                         cache_mem_v[l].reshape(db, n_mem, wd), q_rows, "sample")[:, :n_new]
    xs1 = _merge(x_sample.reshape(ms, d), so_sb.reshape(ms, wd), so_diff.reshape(ms, wd),
                 so_mem.reshape(ms, wd), g_mix, w_gate, b_gate[l], w_branch_b, w_out_b, ms)
    xs1_tm = xs1.reshape(db, n_new, d).transpose(1, 0, 2).reshape(ms, d)
    prefix_tm = state_conv[l].transpose(1, 0, 2)
    ys_tm, a_tm = _ffn_sample(xs1_tm, g_ffn, prefix_tm, w_up_b, cw, cb, w_down_b, n_new, db)
    y_sample = ys_tm.reshape(n_new, db, d).transpose(1, 0, 2)
    a_ext = jnp.concatenate([prefix_tm, a_tm.reshape(n_new, db, d_ff)], axis=0)
    conv_s = a_ext[n_new:].transpose(1, 0, 2)

    st = lambda a, *shape: a.reshape((1,) + shape)
    return (
        y_prompt, y_sample,
        st(sbk_f, bsz, seq, SB_HEADS, HEAD64), st(sbv_f, bsz, seq, SB_HEADS, HEAD64),
        st(dk_f, bsz, seq, DIFF_HEADS, 2, HEAD64), st(dv_f, bsz, seq, DIFF_HEADS, HEAD128),
        st(mk_f, bsz, n_mem, MEM_HEADS, HEAD128), st(mv_f, bsz, n_mem, MEM_HEADS, HEAD128),
        st(conv_tail[:, SUBLANES - (CONV_WIDTH - 1):], bsz, CONV_WIDTH - 1, d_ff),
        st(ssbk_f, db, n_new, SB_HEADS, HEAD64), st(ssbv_f, db, n_new, SB_HEADS, HEAD64),
        st(sdk_f, db, n_new, DIFF_HEADS, 2, HEAD64), st(sdv_f, db, n_new, DIFF_HEADS, HEAD128),
        st(conv_s, db, CONV_WIDTH - 1, d_ff),
    )
```

```python
import functools
import math

import jax
import jax.numpy as jnp
from jax import lax
from jax.experimental import pallas as pl
from jax.experimental.pallas import tpu as pltpu

F32 = jnp.float32
BF16 = jnp.bfloat16

D_MODEL = 1024
SB_HEADS = 8
DIFF_HEADS = 4
MEM_HEADS = 4
HEAD64 = 64
HEAD128 = 128
BRANCH_WIDTH = 512
N_BRANCH = 3
PAGE_SIZE = 128
CONV_WIDTH = 3
ROPE_THETA = 10000.0
EPS = 1e-6
SEC_SB_Q, SEC_SB_K, SEC_SB_V, SEC_D_Q, SEC_D_K, SEC_D_V, SEC_M_Q = range(7)
N_ATTN_SECTIONS = 7

LANES = 128
SUBLANES = 8
MXU_WIDTH = 256
VMEM_LIMIT = 56 * 1024 * 1024
KV_CHUNK = MXU_WIDTH
MAX_PAGES_PER_STEP = 8
DIFF_CHUNKS_PER_STEP = 4

NEG_BIG = -1e30
SB_DONE_LOG = -110.0

_NT = (((1,), (1,)), ((), ()))


def _dot(a, b):
    return jnp.dot(a, b, preferred_element_type=F32)


def _dot_nt(a, b):
    return lax.dot_general(a, b, _NT, preferred_element_type=F32)


def _split(x):
    hi = x.astype(BF16)
    return hi, (x - hi.astype(F32)).astype(BF16)


def _split_dot(x, m):
    hi, lo = _split(x)
    return _dot(hi, m) + _dot(lo, m)


def _rms_rows(x, g):
    ms = jnp.mean(x * x, axis=-1, keepdims=True)
    return x * lax.rsqrt(ms + EPS) * g


def _tile_lanes(x, width):
    reps = width // x.shape[-1]
    return x if reps == 1 else jnp.concatenate([x] * reps, axis=-1)


def _cparams(sem, limit=VMEM_LIMIT):
    return pltpu.CompilerParams(dimension_semantics=sem, vmem_limit_bytes=limit)


def _const_spec(shape):
    nd = len(shape)
    return pl.BlockSpec(shape, lambda *_: (0,) * nd, pipeline_mode=pl.Buffered(1))


def _in_proj_kernel(x_ref, g_ref, wrow_ref, wt_ref, cos_ref, sin_ref, cost_ref, sint_ref,
                    seg64_ref, seg128_ref, dqn_ref, dkn_ref, mqn_ref,
                    sbq_b, dq_b, dv_f, dv_b, mq_b, sbkt_f, sbkt_b, sbvt_f, sbvt_b, dkt_f, dkt_b):
    h = _rms_rows(x_ref[0], g_ref[...]).astype(BF16)
    tm = h.shape[0]
    wd = BRANCH_WIDTH

    def proj(s):
        return _dot(h, wrow_ref[:, s * wd:(s + 1) * wd])

    def proj_t(s):
        return _dot_nt(wt_ref[s * wd:(s + 1) * wd, :], h)

    def put_t(f_ref, b_ref, yt):
        f_ref[0] = yt
        for c in range(tm // KV_CHUNK):
            b_ref[0, c] = yt[:, c * KV_CHUNK:(c + 1) * KV_CHUNK].astype(BF16)

    scale64 = HEAD64 ** -0.5
    half = HEAD64 // 2

    sbq_b[0] = (proj(0) * scale64).astype(BF16)

    y = proj(1)
    y = y * lax.rsqrt(_split_dot(y * y, seg64_ref[...]) + EPS) * dqn_ref[...]
    lane = lax.broadcasted_iota(jnp.int32, y.shape, 1)
    swapped = jnp.where((lane % HEAD64) < half, pltpu.roll(y, wd - half, 1), pltpu.roll(y, half, 1))
    y = y * _tile_lanes(cos_ref[...], wd) + swapped * _tile_lanes(sin_ref[...], wd)
    dq_b[0] = (y * scale64).astype(BF16)

    y = proj(2)
    dv_f[0] = y
    dv_b[0] = y.astype(BF16)

    y = proj(3)
    y = y * lax.rsqrt(_split_dot(y * y, seg128_ref[...]) + EPS) * mqn_ref[...]
    mq_b[0] = y.astype(BF16)

    put_t(sbkt_f, sbkt_b, proj_t(0))
    put_t(sbvt_f, sbvt_b, proj_t(1))

    yt = proj_t(2)
    hi, lo = _split(yt * yt)
    ms = _dot(seg64_ref[...], hi) + _dot(seg64_ref[...], lo)
    yt = yt * lax.rsqrt(ms + EPS) * _tile_lanes(dkn_ref[...], tm)
    parts = []
    for grp in range(wd // HEAD64):
        lo_rows = yt[grp * HEAD64:grp * HEAD64 + half]
        hi_rows = yt[grp * HEAD64 + half:(grp + 1) * HEAD64]
        parts += [hi_rows, lo_rows]
    swapped_t = jnp.concatenate(parts, axis=0)
    reps = wd // HEAD64
    cos_t = jnp.concatenate([cost_ref[...]] * reps, axis=0)
    sin_t = jnp.concatenate([sint_ref[...]] * reps, axis=0)
    put_t(dkt_f, dkt_b, yt * cos_t + swapped_t * sin_t)


def _in_proj(x3d, g, w_row, w_t, cos_tab, sin_tab, cos_t, sin_t, seg64, seg128, dqn, dkn_col, mqn, tm):
    b, t, d = x3d.shape
    wd = BRANCH_WIDTH
    nt = t // tm
    nc = tm // KV_CHUNK
    row_spec = pl.BlockSpec((1, tm, wd), lambda bi, ti: (bi, ti, 0))
    t_spec = pl.BlockSpec((1, wd, tm), lambda bi, ti: (bi, 0, ti))
    tb_spec = pl.BlockSpec((1, nc, wd, KV_CHUNK), lambda bi, ti: (bi, ti, 0, 0))
    row = lambda dt: jax.ShapeDtypeStruct((b, t, wd), dt)
    tr_f = jax.ShapeDtypeStruct((b, wd, t), F32)
    tr_b = jax.ShapeDtypeStruct((b, t // KV_CHUNK, wd, KV_CHUNK), BF16)
    return pl.pallas_call(
        _in_proj_kernel,
        out_shape=(row(BF16), row(BF16), row(F32), row(BF16), row(BF16),
                   tr_f, tr_b, tr_f, tr_b, tr_f, tr_b),
        grid=(b, nt),
        in_specs=[
            pl.BlockSpec((1, tm, d), lambda bi, ti: (bi, ti, 0)),
            _const_spec(g.shape),
            _const_spec(w_row.shape),
            _const_spec(w_t.shape),
            pl.BlockSpec((tm, LANES), lambda bi, ti: (ti, 0)),
            pl.BlockSpec((tm, LANES), lambda bi, ti: (ti, 0)),
            pl.BlockSpec((HEAD64, tm), lambda bi, ti: (0, ti)),
            pl.BlockSpec((HEAD64, tm), lambda bi, ti: (0, ti)),
            _const_spec(seg64.shape),
            _const_spec(seg128.shape),
            _const_spec(dqn.shape),
            _const_spec(dkn_col.shape),
            _const_spec(mqn.shape),
        ],
        out_specs=(row_spec, row_spec, row_spec, row_spec, row_spec,
                   t_spec, tb_spec, t_spec, tb_spec, t_spec, tb_spec),
        compiler_params=_cparams(("parallel", "parallel")),
        name="in_proj",
    )(x3d, g, w_row, w_t, cos_tab, sin_tab, cos_t, sin_t, seg64, seg128, dqn, dkn_col, mqn)


def _mem_kv_kernel(x_ref, g_ref, w_ref, seg128_ref, kn_ref, mk_f, mk_b, mv_f, mv_b):
    h = _rms_rows(x_ref[...], g_ref[...]).astype(BF16)
    wd = BRANCH_WIDTH
    k = _dot(h, w_ref[:, :wd])
    k = k * lax.rsqrt(_split_dot(k * k, seg128_ref[...]) + EPS) * kn_ref[...]
    mk_f[...] = k
    mk_b[...] = k.astype(BF16)
    v = _dot(h, w_ref[:, wd:])
    mv_f[...] = v
    mv_b[...] = v.astype(BF16)


def _mem_kv(mem2d, g, w_kv, seg128, kn):
    m = mem2d.shape[0]
    wd = BRANCH_WIDTH
    full = lambda shape: pl.BlockSpec(shape, lambda i: (0, 0))
    return pl.pallas_call(
        _mem_kv_kernel,
        out_shape=tuple(jax.ShapeDtypeStruct((m, wd), dt) for dt in (F32, BF16, F32, BF16)),
        grid=(1,),
        in_specs=[full(mem2d.shape), full(g.shape), full(w_kv.shape), full(seg128.shape),
                  full(kn.shape)],
        out_specs=tuple(full((m, wd)) for _ in range(4)),
        compiler_params=_cparams(("arbitrary",)),
        name="mem_kv",
    )(mem2d, g, w_kv, seg128, kn)


def _sb_logs(z, vis):
    sp = jnp.log1p(jnp.exp(-jnp.abs(z)))
    log_keep = -(jnp.maximum(z, 0.0) + sp)
    log_take = jnp.minimum(z, 0.0) - sp
    if vis is not None:
        log_keep = jnp.where(vis, log_keep, 0.0)
    return log_keep, log_take


def _sb_weights(log_take, excl, c, vis):
    w = jnp.exp(log_take + excl + _tile_lanes(c, excl.shape[-1]))
    return w if vis is None else jnp.where(vis, w, 0.0)


def _sb_prompt_kernel(q_ref, kt_ref, vt_ref, u_ref, o_ref, acc_ref, c_ref, *, tq):
    qi = pl.program_id(2)
    q = q_ref[0]
    lane = lax.broadcasted_iota(jnp.int32, (tq, LANES), 1)
    first = lane < HEAD64
    zero = jnp.zeros_like(q)
    q_heads = (jnp.where(first, q, zero), jnp.where(first, zero, q))
    acc_ref[...] = jnp.zeros_like(acc_ref)
    c_ref[...] = jnp.zeros_like(c_ref)
    tk = KV_CHUNK

    def tile(kc, masked):
        kt = kt_ref[0, kc]
        vt = vt_ref[0, kc]
        vis = None
        if masked:
            vis = (lax.broadcasted_iota(jnp.int32, (tq, tk), 1)
                   < lax.broadcasted_iota(jnp.int32, (tq, tk), 0))
        pv = []
        for hh in range(2):
            z = _dot(q_heads[hh], kt)
            log_keep, log_take = _sb_logs(z, vis)
            ee = _split_dot(log_keep, u_ref[...])
            c = c_ref[hh]
            w = _sb_weights(log_take, ee[:, :tk], c, vis)
            pv.append(_dot_nt(w.astype(BF16), vt))
            c_ref[hh] = c + ee[:, tk:]
        acc_ref[...] += jnp.where(first, pv[0], pv[1])

    tile(qi, True)

    def cond(state):
        kc, done = state
        return jnp.logical_and(kc >= 0, done == 0)

    def body(state):
        kc, _ = state
        tile(kc, False)
        done = (jnp.max(c_ref[...]) < SB_DONE_LOG).astype(jnp.int32)
        return kc - 1, done

    lax.while_loop(cond, body, (qi - 1, jnp.int32(0)))
    o_ref[0] = acc_ref[...].astype(o_ref.dtype)


def _reverse_cumsum_matrix(tk):
    j = jnp.arange(tk)
    u = (j[:, None] > j[None, :]).astype(BF16)
    return jnp.concatenate([u, jnp.ones((tk, LANES), BF16)], axis=1)


def _sb_prompt(q, kt, vt):
    b, t, wd = q.shape
    n_chunks = kt.shape[1]
    tq = KV_CHUNK
    kernel = functools.partial(_sb_prompt_kernel, tq=tq)
    qspec = pl.BlockSpec((1, tq, LANES), lambda bi, hp, qi: (bi, qi, hp))
    kvspec = pl.BlockSpec((1, n_chunks, LANES, KV_CHUNK), lambda bi, hp, qi: (bi, 0, hp, 0))
    return pl.pallas_call(
        kernel,
        out_shape=jax.ShapeDtypeStruct((b, t, wd), BF16),
        grid=(b, wd // LANES, t // tq),
        in_specs=[qspec, kvspec, kvspec, _const_spec((KV_CHUNK, KV_CHUNK + LANES))],
        out_specs=qspec,
        scratch_shapes=[pltpu.VMEM((tq, LANES), F32), pltpu.VMEM((2, tq, LANES), F32)],
        compiler_params=_cparams(("parallel", "parallel", "parallel")),
        name="sb_prompt",
    )(q, kt, vt, _reverse_cumsum_matrix(KV_CHUNK))


def _lambda_value(lam_ref, lam_init):
    lv = lam_ref[...]
    a = jnp.sum(lv[0:1] * lv[1:2], axis=-1, keepdims=True)
    b = jnp.sum(lv[2:3] * lv[3:4], axis=-1, keepdims=True)
    return jnp.exp(a) - jnp.exp(b) + lam_init


def _diff_prompt_kernel(lam_ref, sub_ref, q_ref, kt_ref, v_ref, o_ref, m_ref, acc_ref,
                        *, tq, lam_init):
    qi = pl.program_id(2)
    q = q_ref[0]
    lane = lax.broadcasted_iota(jnp.int32, (tq, LANES), 1)
    first = lane < HEAD64
    zero = jnp.zeros_like(q)
    qs = jnp.concatenate([jnp.where(first, q, zero), jnp.where(first, zero, q)], axis=0)
    m_ref[...] = jnp.full_like(m_ref, NEG_BIG)
    acc_ref[...] = jnp.zeros_like(acc_ref)
    ck = KV_CHUNK
    cps = DIFF_CHUNKS_PER_STEP

    def step(chunks, v, masked):
        width = ck * len(chunks)
        v_ext = jnp.concatenate([v, jnp.ones_like(v)], axis=1)
        s = [_dot(qs, kt_ref[0, c]) for c in chunks]
        s = s[0] if len(s) == 1 else jnp.concatenate(s, axis=1)
        if masked:
            row = lax.broadcasted_iota(jnp.int32, (2 * tq, width), 0) % tq
            s = jnp.where(lax.broadcasted_iota(jnp.int32, (2 * tq, width), 1) <= row, s, NEG_BIG)
        m_prev = m_ref[...]
        m_new = jnp.maximum(m_prev, jnp.max(s, axis=-1, keepdims=True))
        p = jnp.exp(s - _tile_lanes(m_new, width))
        alpha = jnp.exp(m_prev - m_new)
        acc_ref[...] = _tile_lanes(alpha, 2 * LANES) * acc_ref[...] + _dot(p.astype(BF16), v_ext)
        m_ref[...] = m_new

    def v_rows(chunk, n):
        return v_ref[0, pl.ds(pl.multiple_of(chunk * ck, ck), n * ck), :]

    def body(kt, carry):
        step([cps * kt + i for i in range(cps)], v_rows(cps * kt, cps), False)
        return carry

    n_full = qi // cps
    lax.fori_loop(0, n_full, body, 0)

    def body_single(c, carry):
        step([c], v_rows(c, 1), False)
        return carry

    lax.fori_loop(n_full * cps, qi, body_single, 0)
    step([qi], v_rows(qi, 1), True)

    lam = _lambda_value(lam_ref, lam_init)
    a0, a1 = acc_ref[0:tq], acc_ref[tq:2 * tq]
    o = a0[:, :LANES] / a0[:, LANES:] - lam * (a1[:, :LANES] / a1[:, LANES:])
    o = _rms_rows(o, sub_ref[...]) * (1.0 - lam_init)
    o_ref[0] = o.astype(o_ref.dtype)


def _diff_prompt(lam_vecs, subln, q, kt, v, lam_init):
    b, t, wd = q.shape
    n_chunks = kt.shape[1]
    tq = KV_CHUNK
    kernel = functools.partial(_diff_prompt_kernel, tq=tq, lam_init=lam_init)
    qspec = pl.BlockSpec((1, tq, LANES), lambda bi, h, qi: (bi, qi, h))
    ktspec = pl.BlockSpec((1, n_chunks, LANES, KV_CHUNK), lambda bi, h, qi: (bi, 0, h, 0))
    vspec = pl.BlockSpec((1, t, LANES), lambda bi, h, qi: (bi, 0, h))
    return pl.pallas_call(
        kernel,
        out_shape=jax.ShapeDtypeStruct((b, t, wd), BF16),
        grid=(b, wd // LANES, t // tq),
        in_specs=[_const_spec(lam_vecs.shape), _const_spec(subln.shape), qspec, ktspec, vspec],
        out_specs=qspec,
        scratch_shapes=[pltpu.VMEM((2 * tq, LANES), F32), pltpu.VMEM((2 * tq, 2 * LANES), F32)],
        compiler_params=_cparams(("parallel", "parallel", "parallel")),
        name="diff_prompt",
    )(lam_vecs, subln, q, kt, v)


def _softmax_pv(s, v):
    p = jnp.exp(s - jnp.max(s, axis=-1, keepdims=True))
    return _dot(p.astype(BF16), v) / jnp.sum(p, axis=-1, keepdims=True)


def _mem_attend_prompt_kernel(q_ref, k_ref, v_ref, o_ref):
    q, k, v = q_ref[0], k_ref[0], v_ref[0]
    scale = HEAD128 ** -0.5
    outs = []
    for h in range(MEM_HEADS):
        sl = slice(h * HEAD128, (h + 1) * HEAD128)
        outs.append(_softmax_pv(_dot_nt(q[:, sl], k[:, sl]) * scale, v[:, sl]))
    o_ref[0] = jnp.concatenate(outs, axis=-1).astype(o_ref.dtype)


def _mem_attend_prompt(q, k, v, tq):
    g, r, wd = q.shape
    n_mem = k.shape[1]
    qspec = pl.BlockSpec((1, tq, wd), lambda gi, ri: (gi, ri, 0))
    kvspec = pl.BlockSpec((1, n_mem, wd), lambda gi, ri: (gi, 0, 0))
    return pl.pallas_call(
        _mem_attend_prompt_kernel,
        out_shape=jax.ShapeDtypeStruct((g, r, wd), BF16),
        grid=(g, r // tq),
        in_specs=[qspec, kvspec, kvspec],
        out_specs=qspec,
        compiler_params=_cparams(("parallel", "parallel")),
        name="mem_attend_prompt",
    )(q, k, v)


def _mem_attend_sample_kernel(q_ref, k_ref, v_ref, o_ref, *, n_mem):
    q = q_ref[0]
    scale = HEAD128 ** -0.5
    outs = []
    for h in range(MEM_HEADS):
        sl = slice(h * HEAD128, (h + 1) * HEAD128)
        k = k_ref[0, pl.ds(h, n_mem, stride=MEM_HEADS), :].astype(BF16)
        v = v_ref[0, pl.ds(h, n_mem, stride=MEM_HEADS), :].astype(BF16)
        outs.append(_softmax_pv(_dot_nt(q[:, sl], k) * scale, v))
    o_ref[0] = jnp.concatenate(outs, axis=-1).astype(o_ref.dtype)


def _mem_attend_sample(q, k_rows, v_rows, n_mem):
    db, r, wd = q.shape
    qspec = pl.BlockSpec((1, r, wd), lambda bi: (bi, 0, 0))
    kvspec = pl.BlockSpec((1, n_mem * MEM_HEADS, HEAD128), lambda bi: (bi, 0, 0))
    return pl.pallas_call(
        functools.partial(_mem_attend_sample_kernel, n_mem=n_mem),
        out_shape=jax.ShapeDtypeStruct((db, r, wd), BF16),
        grid=(db,),
        in_specs=[qspec, kvspec, kvspec],
        out_specs=qspec,
        compiler_params=_cparams(("parallel",)),
        name="mem_attend_sample",
    )(q, k_rows, v_rows)


def _sb_sample_kernel(pt_ref, q_ref, newk_ref, newv_ref, *rest, n_new, g_pages):
    k_refs, v_refs = rest[:g_pages], rest[g_pages:2 * g_pages]
    u_ref, gmask_ref, o_ref, acc_ref, c_ref = rest[2 * g_pages:]
    j = pl.program_id(1)
    q = q_ref[0]
    rows = q.shape[0]

    def pages(ks, vs, vis):
        n = len(ks)
        logs = [_sb_logs(_dot(q, k[0].astype(BF16)), vis) for k in ks]
        log_keep = logs[0][0] if n == 1 else jnp.concatenate([lg[0] for lg in logs], axis=0)
        ee = _split_dot(log_keep, u_ref[...])
        c = c_ref[...]
        acc = acc_ref[...]
        for i in range(n):
            e = ee[i * rows:(i + 1) * rows]
            w = _sb_weights(logs[i][1], e[:, :PAGE_SIZE], c, vis)
            acc = acc + _dot_nt(w.astype(BF16), vs[i][0].astype(BF16))
            c = c + e[:, PAGE_SIZE:]
        acc_ref[...] = acc
        c_ref[...] = c

    @pl.when(j == 0)
    def _():
        acc_ref[...] = jnp.zeros_like(acc_ref)
        c_ref[...] = jnp.zeros_like(c_ref)
        t = lax.broadcasted_iota(jnp.int32, (rows, PAGE_SIZE), 0) // SUBLANES
        s = lax.broadcasted_iota(jnp.int32, (rows, PAGE_SIZE), 1)
        pages([newk_ref], [newv_ref], s < t)

    pages(k_refs, v_refs, None)

    @pl.when(j == pl.num_programs(1) - 1)
    def _():
        masked = acc_ref[...] * gmask_ref[...]
        o = jnp.sum(masked.reshape(n_new, SUBLANES, masked.shape[-1]), axis=1)
        o_ref[0] = o.astype(o_ref.dtype)


def _diff_sample_kernel(pt_ref, lam_ref, sub_ref, q_ref, newk_ref, newv_ref, *rest,
                        n_new, g_pages, lam_init):
    k_refs, v_refs = rest[:g_pages], rest[g_pages:2 * g_pages]
    o_ref, m_ref, l_ref, acc_ref = rest[2 * g_pages:]
    j = pl.program_id(1)
    q = q_ref[0]
    rows = q.shape[0]

    def pages(ks, vs, vis):
        n = len(ks)
        s = [_dot(q, k[0].astype(BF16)) for k in ks]
        s = s[0] if n == 1 else jnp.concatenate(s, axis=1)
        if vis is not None:
            s = jnp.where(vis, s, NEG_BIG)
        m_prev = m_ref[...]
        m_new = jnp.maximum(m_prev, jnp.max(s, axis=-1, keepdims=True))
        p = jnp.exp(s - _tile_lanes(m_new, n * PAGE_SIZE))
        alpha = jnp.exp(m_prev - m_new)
        l_ref[...] = alpha * l_ref[...] + jnp.sum(p, axis=-1, keepdims=True)
        pv = []
        for h in range(DIFF_HEADS):
            ph = p[h * SUBLANES:(h + 1) * SUBLANES].astype(BF16)
            tot = None
            for i in range(n):
                vh = vs[i][0, pl.ds(h, PAGE_SIZE, stride=DIFF_HEADS), :].astype(BF16)
                term = _dot(ph[:, i * PAGE_SIZE:(i + 1) * PAGE_SIZE], vh)
                tot = term if tot is None else tot + term
            pv.append(tot)
        acc_ref[...] = alpha * acc_ref[...] + jnp.concatenate(pv, axis=0)
        m_ref[...] = m_new

    @pl.when(j == 0)
    def _():
        m_ref[...] = jnp.full_like(m_ref, NEG_BIG)
        l_ref[...] = jnp.zeros_like(l_ref)
        acc_ref[...] = jnp.zeros_like(acc_ref)
        t = (lax.broadcasted_iota(jnp.int32, (rows, PAGE_SIZE), 0) % SUBLANES) // 2
        s = lax.broadcasted_iota(jnp.int32, (rows, PAGE_SIZE), 1)
        pages([newk_ref], [newv_ref], s <= t)

    pages(k_refs, v_refs, None)

    @pl.when(j == pl.num_programs(1) - 1)
    def _():
        lam = _lambda_value(lam_ref, lam_init)
        r = lax.broadcasted_iota(jnp.int32, (rows, LANES), 0)
        coef = jnp.where(r % 2 == 0, 1.0, -lam)
        acc_ref[...] = acc_ref[...] * (coef / l_ref[...])
        outs = []
        for h in range(DIFF_HEADS):
            o = (acc_ref[pl.ds(h * SUBLANES, n_new, stride=2), :]
                 + acc_ref[pl.ds(h * SUBLANES + 1, n_new, stride=2), :])
            outs.append(_rms_rows(o, sub_ref[...]) * (1.0 - lam_init))
        o_ref[0] = jnp.concatenate(outs, axis=-1).astype(o_ref.dtype)


def _pages_per_step(n_pages):
    return max(g for g in range(1, MAX_PAGES_PER_STEP + 1) if n_pages % g == 0)


def _paged_specs(n_pages, g_pages):
    per_b = lambda shape: pl.BlockSpec(shape, lambda b, j, pt: (b, 0, 0))
    const = lambda shape: pl.BlockSpec(shape, lambda b, j, pt: (0,) * len(shape))

    def page(g):
        return pl.BlockSpec((1, BRANCH_WIDTH, PAGE_SIZE),
                            lambda b, j, pt: (pt[b, n_pages - 1 - (j * g_pages + g)], 0, 0))

    return per_b, const, [page(g) for g in range(g_pages)]


def _sb_sample(page_table, qbd, newk, newv, pool_k, pool_v, gmask, n_new):
    db, rows, wd = qbd.shape
    n_pages = page_table.shape[1]
    g_pages = _pages_per_step(n_pages)
    per_b, const, page_specs = _paged_specs(n_pages, g_pages)
    uext = _reverse_cumsum_matrix(PAGE_SIZE)
    grid_spec = pltpu.PrefetchScalarGridSpec(
        num_scalar_prefetch=1,
        grid=(db, n_pages // g_pages),
        in_specs=[per_b((1, rows, wd)), per_b((1, wd, PAGE_SIZE)), per_b((1, wd, PAGE_SIZE))]
                 + page_specs + page_specs + [const(uext.shape), const(gmask.shape)],
        out_specs=per_b((1, n_new, wd)),
        scratch_shapes=[pltpu.VMEM((rows, wd), F32), pltpu.VMEM((rows, LANES), F32)],
    )
    return pl.pallas_call(
        functools.partial(_sb_sample_kernel, n_new=n_new, g_pages=g_pages),
        out_shape=jax.ShapeDtypeStruct((db, n_new, wd), F32),
        grid_spec=grid_spec,
        compiler_params=_cparams(("parallel", "arbitrary")),
        name="sb_sample",
    )(page_table, qbd, newk, newv, *([pool_k] * g_pages), *([pool_v] * g_pages), uext, gmask)


def _diff_sample(page_table, lam_vecs, subln, qbd, newk, newv, pool_k, pool_v, n_new, lam_init):
    db, rows, wd = qbd.shape
    n_pages = page_table.shape[1]
    g_pages = _pages_per_step(n_pages)
    per_b, const, page_specs = _paged_specs(n_pages, g_pages)
    grid_spec = pltpu.PrefetchScalarGridSpec(
        num_scalar_prefetch=1,
        grid=(db, n_pages // g_pages),
        in_specs=[const(lam_vecs.shape), const(subln.shape),
                  per_b((1, rows, wd)), per_b((1, wd, PAGE_SIZE)), per_b((1, wd, PAGE_SIZE))]
                 + page_specs + page_specs,
        out_specs=per_b((1, n_new, wd)),
        scratch_shapes=[pltpu.VMEM((rows, LANES), F32), pltpu.VMEM((rows, LANES), F32),
                        pltpu.VMEM((rows, LANES), F32)],
    )
    return pl.pallas_call(
        functools.partial(_diff_sample_kernel, n_new=n_new, g_pages=g_pages, lam_init=lam_init),
        out_shape=jax.ShapeDtypeStruct((db, n_new, wd), F32),
        grid_spec=grid_spec,
        compiler_params=_cparams(("parallel", "arbitrary")),
        name="diff_sample",
    )(page_table, lam_vecs, subln, qbd, newk, newv, *([pool_k] * g_pages), *([pool_v] * g_pages))


def _merge_kernel(x_ref, osb_ref, odf_ref, omem_ref, g_ref, wg_ref, bg_ref, wb_ref, wo_ref,
                  o_ref):
    x = x_ref[...]
    h = _rms_rows(x, g_ref[...]).astype(BF16)
    merged = None
    for b, ob_ref in enumerate((osb_ref, odf_ref, omem_ref)):
        gate = _dot(h, wg_ref[:, b * D_MODEL:(b + 1) * D_MODEL]) + bg_ref[b:b + 1, :]
        term = jax.nn.sigmoid(gate) * _dot(ob_ref[...].astype(BF16), wb_ref[b])
        merged = term if merged is None else merged + term
    o_ref[...] = x + _dot(merged.astype(BF16), wo_ref[...])


def _merge(x2d, o_sb, o_diff, o_mem, g, w_gate, b_gate, w_branch, w_out, tm):
    m = x2d.shape[0]
    row = lambda i: (i, 0)
    wide = pl.BlockSpec((tm, D_MODEL), row)
    narrow = pl.BlockSpec((tm, BRANCH_WIDTH), row)
    return pl.pallas_call(
        _merge_kernel,
        out_shape=jax.ShapeDtypeStruct((m, D_MODEL), F32),
        grid=(m // tm,),
        in_specs=[wide, narrow, narrow, narrow, _const_spec(g.shape), _const_spec(w_gate.shape),
                  _const_spec(b_gate.shape), _const_spec(w_branch.shape),
                  _const_spec(w_out.shape)],
        out_specs=wide,
        compiler_params=_cparams(("parallel",)),
        name="merge",
    )(x2d, o_sb, o_diff, o_mem, g, w_gate, b_gate, w_branch, w_out)


def _ffn_chunks(d_ff):
    chunk = MXU_WIDTH if d_ff % MXU_WIDTH == 0 else LANES
    return chunk, d_ff // chunk


def _ffn_prompt_kernel(x_ref, g_ref, wup_ref, cw_ref, cb_ref, wdn_ref, o_ref, tail_ref, abuf_ref,
                       *, tm, d_ff):
    pad = SUBLANES

    @pl.when(pl.program_id(1) == 0)
    def _():
        abuf_ref[0:pad, :] = jnp.zeros((pad, d_ff), F32)

    x = x_ref[0]
    h = _rms_rows(x, g_ref[...]).astype(BF16)
    chunk, n_chunks = _ffn_chunks(d_ff)
    acc = x
    for c in range(n_chunks):
        sl = slice(c * chunk, (c + 1) * chunk)
        abuf_ref[pad:pad + tm, sl] = _dot(h, wup_ref[:, sl])
        conv = (cb_ref[:, sl]
                + cw_ref[0:1, sl] * abuf_ref[pad - 2:pad - 2 + tm, sl]
                + cw_ref[1:2, sl] * abuf_ref[pad - 1:pad - 1 + tm, sl]
                + cw_ref[2:3, sl] * abuf_ref[pad:pad + tm, sl])
        up = _dot(h, wup_ref[:, d_ff + c * chunk:d_ff + (c + 1) * chunk])
        acc = acc + _dot((jax.nn.silu(conv) * up).astype(BF16), wdn_ref[sl, :])
    o_ref[0] = acc
    last = abuf_ref[tm:tm + pad, :]
    tail_ref[0] = last
    abuf_ref[0:pad, :] = last


def _ffn_prompt(x3d, g, w_up, conv_w, conv_b, w_down, tm):
    b, t, d = x3d.shape
    d_ff = w_down.shape[0]
    kernel = functools.partial(_ffn_prompt_kernel, tm=tm, d_ff=d_ff)
    xspec = pl.BlockSpec((1, tm, d), lambda bi, ti: (bi, ti, 0))
    return pl.pallas_call(
        kernel,
        out_shape=(jax.ShapeDtypeStruct((b, t, d), F32),
                   jax.ShapeDtypeStruct((b, SUBLANES, d_ff), F32)),
        grid=(b, t // tm),
        in_specs=[xspec, _const_spec(g.shape), _const_spec(w_up.shape), _const_spec(conv_w.shape),
                  _const_spec(conv_b.shape), _const_spec(w_down.shape)],
        out_specs=(xspec, pl.BlockSpec((1, SUBLANES, d_ff), lambda bi, ti: (bi, 0, 0))),
        scratch_shapes=[pltpu.VMEM((tm + SUBLANES, d_ff), F32)],
        compiler_params=_cparams(("parallel", "arbitrary")),
        name="ffn_prompt",
    )(x3d, g, w_up, conv_w, conv_b, w_down)


def _ffn_sample_kernel(x_ref, g_ref, pre_ref, wa_ref, wu_ref, cw_ref, cb_ref, wdn_ref,
                       o_ref, a_ref, h_ref, *, n_new, db):
    c = pl.program_id(0)

    @pl.when(c == 0)
    def _():
        x = x_ref[...]
        h_ref[...] = _rms_rows(x, g_ref[...]).astype(BF16)
        o_ref[...] = x

    h = h_ref[...]
    a = _dot(h, wa_ref[...])
    a_ref[...] = a
    up = _dot(h, wu_ref[...])
    slabs = [pre_ref[0], pre_ref[1]] + [a[t * db:(t + 1) * db] for t in range(n_new)]
    conv = jnp.concatenate(
        [cb_ref[...] + cw_ref[0:1] * slabs[t] + cw_ref[1:2] * slabs[t + 1] + cw_ref[2:3] * slabs[t + 2]
         for t in range(n_new)], axis=0)
    o_ref[...] += _dot((jax.nn.silu(conv) * up).astype(BF16), wdn_ref[...])


def _ffn_sample(x_tm, g, prefix_tm, w_up, conv_w, conv_b, w_down, n_new, db):
    m, d = x_tm.shape
    d_ff = w_down.shape[0]
    chunk, n_chunks = _ffn_chunks(d_ff)
    kernel = functools.partial(_ffn_sample_kernel, n_new=n_new, db=db)
    fixed = lambda shape: pl.BlockSpec(shape, lambda c: (0,) * len(shape))
    return pl.pallas_call(
        kernel,
        out_shape=(jax.ShapeDtypeStruct((m, d), F32), jax.ShapeDtypeStruct((m, d_ff), F32)),
        grid=(n_chunks,),
        in_specs=[fixed((m, d)), fixed(g.shape),
                  pl.BlockSpec((CONV_WIDTH - 1, db, chunk), lambda c: (0, 0, c)),
                  pl.BlockSpec((d, chunk), lambda c: (0, c)),
                  pl.BlockSpec((d, chunk), lambda c: (0, n_chunks + c)),
                  pl.BlockSpec((CONV_WIDTH, chunk), lambda c: (0, c)),
                  pl.BlockSpec((1, chunk), lambda c: (0, c)),
                  pl.BlockSpec((chunk, d), lambda c: (c, 0))],
        out_specs=(fixed((m, d)), pl.BlockSpec((m, chunk), lambda c: (0, c))),
        scratch_shapes=[pltpu.VMEM((m, d), BF16)],
        compiler_params=_cparams(("arbitrary",)),
        name="ffn_sample",
    )(x_tm, g, prefix_tm, w_up, w_up, conv_w, conv_b, w_down)


def _rope_tables(pos):
    half = HEAD64 // 2
    inv_freq = jnp.power(ROPE_THETA, -jnp.arange(half, dtype=F32) / half)
    ang = pos.astype(F32)[:, None] * inv_freq[None, :]
    cos, sin = jnp.cos(ang), jnp.sin(ang)
    cos = jnp.concatenate([cos, cos], axis=-1)
    sin = jnp.concatenate([-sin, sin], axis=-1)
    reps = LANES // HEAD64
    return jnp.tile(cos, (1, reps)), jnp.tile(sin, (1, reps)), cos.T, sin.T


def _seg_matrix(width, seg):
    i = jnp.arange(width) // seg
    return ((i[:, None] == i[None, :]).astype(F32) / seg).astype(BF16)


def _tile_gain(gain, width):
    return jnp.tile(gain.astype(F32), width // gain.shape[0]).reshape(1, width)


def _block_diag_queries(q, t_idx, grp):
    lane_grp = jnp.arange(q.shape[-1]) // HEAD64
    mask = (lane_grp[None, :] == grp[:, None]).astype(q.dtype)
    return jnp.take(q, t_idx, axis=1) * mask[None]


def _state_from_t(yt, lead, heads):
    n = len(heads)
    y = yt.reshape(tuple(heads) + tuple(lead))
    perm = tuple(range(n, n + len(lead))) + tuple(range(n))
    return jnp.transpose(y, perm)


def kernel(x_prompt, x_sample, cache_sb_k, cache_sb_v, cache_diff_k, cache_diff_v, cache_mem_k, cache_mem_v, state_conv, page_table, mem_prompt, norm_mix, norm_mem, w_in, b_gate, diff_q_norm, diff_k_norm, lambda_q1, lambda_k1, lambda_q2, lambda_k2, diff_subln, w_mem_kv, mem_q_norm, mem_k_norm, w_branch, w_out, norm_ffn, w_ffn_up, conv_w, conv_b, w_ffn_down):
    depth = w_in.shape[0]
    assert depth == 1, "single-layer step"
    l = 0
    lam_init = 0.8 - 0.6 * math.exp(-0.3 * l)
    bsz, seq, d = x_prompt.shape
    db, n_new, _ = x_sample.shape
    assert 2 * n_new == SUBLANES, "sample query rows are packed eight (t, map) rows per head"
    n_pages = page_table.shape[1]
    n_pool = cache_sb_k.shape[1]
    past_len = n_pages * PAGE_SIZE
    n_mem = mem_prompt.shape[1]
    wd = BRANCH_WIDTH
    d_ff = w_ffn_down.shape[1]
    n_attn = N_ATTN_SECTIONS * wd
    ms = db * n_new

    w_in_b = w_in[l].astype(BF16)
    sec = lambda s: w_in_b[:, s * wd:(s + 1) * wd]
    w_row = jnp.concatenate([sec(SEC_SB_Q), sec(SEC_D_Q), sec(SEC_D_V), sec(SEC_M_Q)], axis=1)
    w_t = jnp.concatenate([sec(SEC_SB_K), sec(SEC_SB_V), sec(SEC_D_K)], axis=1).T
    w_gate = w_in_b[:, n_attn:]
    w_branch_b = w_branch[l].astype(BF16)
    w_out_b = w_out[l].astype(BF16)
    w_up_b = w_ffn_up[l].astype(BF16)
    w_down_b = w_ffn_down[l].astype(BF16)
    w_kv_b = w_mem_kv[l].astype(BF16)
    g_mix = norm_mix[l].reshape(1, d)
    g_mem = norm_mem[l].reshape(1, d)
    g_ffn = norm_ffn[l].reshape(1, d)
    seg64, seg128 = _seg_matrix(wd, HEAD64), _seg_matrix(wd, HEAD128)
    dqn = _tile_gain(diff_q_norm[l], wd)
    dkn_col = jnp.broadcast_to(_tile_gain(diff_k_norm[l], wd).reshape(wd, 1), (wd, LANES))
    mqn, mkn = _tile_gain(mem_q_norm[l], wd), _tile_gain(mem_k_norm[l], wd)
    subln = diff_subln[l].reshape(1, HEAD128)
    lam_vecs = jnp.stack([lambda_q1[l], lambda_k1[l], lambda_q2[l], lambda_k2[l]]).astype(F32)
    cw, cb = conv_w[l], conv_b[l].reshape(1, d_ff)

    def in_proj(x3d, pos, tm):
        cos_r, sin_r, cos_c, sin_c = _rope_tables(pos)
        return _in_proj(x3d, g_mix, w_row, w_t, cos_r, sin_r, cos_c, sin_c, seg64, seg128,
                        dqn, dkn_col, mqn, tm)

    tm = min(512, seq)
    (sbq_b, dq_b, dv_f, dv_b, mq_b, sbkt_f, sbkt_b, sbvt_f, sbvt_b, dkt_f, dkt_b) = in_proj(
        x_prompt, jnp.arange(seq), tm)
    mk_f, mk_b, mv_f, mv_b = _mem_kv(mem_prompt.reshape(bsz * n_mem, d), g_mem, w_kv_b, seg128, mkn)

    o_sb = _sb_prompt(sbq_b, sbkt_b, sbvt_b)
    o_diff = _diff_prompt(lam_vecs, subln, dq_b, dkt_b, dv_b, lam_init)
    o_mem = _mem_attend_prompt(mq_b, mk_b.reshape(bsz, n_mem, wd), mv_b.reshape(bsz, n_mem, wd), tm)
    x1 = _merge(x_prompt.reshape(bsz * seq, d), o_sb.reshape(-1, wd), o_diff.reshape(-1, wd),
                o_mem.reshape(-1, wd), g_mix, w_gate, b_gate[l], w_branch_b, w_out_b, tm)
    y_prompt, conv_tail = _ffn_prompt(x1.reshape(bsz, seq, d), g_ffn, w_up_b, cw, cb, w_down_b, tm)

    pos_s = jnp.tile(past_len + jnp.arange(n_new), db)
    (ssbq_b, sdq_b, sdv_f, _, smq_b, ssbkt_f, _, ssbvt_f, _, sdkt_f, _) = in_proj(
        x_sample.reshape(1, ms, d), pos_s, ms)

    s3 = lambda a: a.reshape(db, n_new, wd)

    def new_t_page(yt):
        y = yt.reshape(wd, db, n_new).transpose(1, 0, 2)
        return jnp.pad(y, ((0, 0), (0, 0), (0, PAGE_SIZE - n_new)))

    r = jnp.arange(n_new * SUBLANES)
    t_sb, grp_sb = r // SUBLANES, r % SUBLANES
    t_df, grp_df = (r % SUBLANES) // 2, 2 * (r // SUBLANES) + r % 2
    gmask_sb = ((jnp.arange(wd) // HEAD64)[None, :] == grp_sb[:, None]).astype(F32)

    t_pages = lambda c: jnp.swapaxes(c[l].reshape(n_pool, PAGE_SIZE, wd), 1, 2)
    row_pages = lambda c: c[l].reshape(c.shape[1], -1, HEAD128)

    so_sb = _sb_sample(page_table, _block_diag_queries(s3(ssbq_b), t_sb, grp_sb),
                       new_t_page(ssbkt_f), new_t_page(ssbvt_f),
                       t_pages(cache_sb_k), t_pages(cache_sb_v), gmask_sb, n_new)
    new_dv = jnp.pad(sdv_f.reshape(db, n_new * DIFF_HEADS, HEAD128),
                     ((0, 0), (0, (PAGE_SIZE - n_new) * DIFF_HEADS), (0, 0)))
    so_diff = _diff_sample(page_table, lam_vecs, subln, _block_diag_queries(s3(sdq_b), t_df, grp_df),
                           new_t_page(sdkt_f), new_dv,
                           t_pages(cache_diff_k), row_pages(cache_diff_v), n_new, lam_init)
    q_rows = 2 * SUBLANES
    q_pad = jnp.pad(s3(smq_b), ((0, 0), (0, q_rows - n_new), (0, 0)))
    so_mem = _mem_attend_sample(q_pad, row_pages(cache_mem_k), row_pages(cache_mem_v), n_mem)[:, :n_new]
    xs1 = _merge(x_sample.reshape(ms, d), so_sb.reshape(ms, wd), so_diff.reshape(ms, wd),
                 so_mem.reshape(ms, wd), g_mix, w_gate, b_gate[l], w_branch_b, w_out_b, ms)
    xs1_tm = xs1.reshape(db, n_new, d).transpose(1, 0, 2).reshape(ms, d)
    prefix_tm = state_conv[l].transpose(1, 0, 2)
    ys_tm, a_tm = _ffn_sample(xs1_tm, g_ffn, prefix_tm, w_up_b, cw, cb, w_down_b, n_new, db)
    y_sample = ys_tm.reshape(n_new, db, d).transpose(1, 0, 2)
    a_ext = jnp.concatenate([prefix_tm, a_tm.reshape(n_new, db, d_ff)], axis=0)
    conv_s = a_ext[n_new:].transpose(1, 0, 2)

    st = lambda a, *shape: a.reshape((1,) + shape)
    sb_heads, df_heads = (SB_HEADS, HEAD64), (DIFF_HEADS, 2, HEAD64)
    return (
        y_prompt, y_sample,
        jnp.transpose(sbkt_f.reshape(bsz, SB_HEADS, HEAD64, seq), (0, 3, 1, 2))[None],
        jnp.transpose(sbvt_f.reshape(bsz, SB_HEADS, HEAD64, seq), (0, 3, 1, 2))[None],
        jnp.transpose(dkt_f.reshape(bsz, DIFF_HEADS, 2, HEAD64, seq), (0, 4, 1, 2, 3))[None],
        st(dv_f, bsz, seq, DIFF_HEADS, HEAD128),
        st(mk_f, bsz, n_mem, MEM_HEADS, HEAD128), st(mv_f, bsz, n_mem, MEM_HEADS, HEAD128),
        st(conv_tail[:, SUBLANES - (CONV_WIDTH - 1):], bsz, CONV_WIDTH - 1, d_ff),
        _state_from_t(ssbkt_f, (db, n_new), sb_heads)[None],
        _state_from_t(ssbvt_f, (db, n_new), sb_heads)[None],
        _state_from_t(sdkt_f, (db, n_new), df_heads)[None],
        st(sdv_f, db, n_new, DIFF_HEADS, HEAD128),
        st(conv_s, db, CONV_WIDTH - 1, d_ff),
    )
```

```python
import functools
import math

import jax
import jax.numpy as jnp
from jax import lax
from jax.experimental import pallas as pl
from jax.experimental.pallas import tpu as pltpu

F32 = jnp.float32
BF16 = jnp.bfloat16

D_MODEL = 1024
SB_HEADS = 8
DIFF_HEADS = 4
MEM_HEADS = 4
HEAD64 = 64
HEAD128 = 128
BRANCH_WIDTH = 512
N_BRANCH = 3
PAGE_SIZE = 128
CONV_WIDTH = 3
ROPE_THETA = 10000.0
EPS = 1e-6
SEC_SB_Q, SEC_SB_K, SEC_SB_V, SEC_D_Q, SEC_D_K, SEC_D_V, SEC_M_Q = range(7)
N_ATTN_SECTIONS = 7

LANES = 128
SUBLANES = 8
MXU_WIDTH = 256
VMEM_LIMIT = 56 * 1024 * 1024
KV_CHUNK = MXU_WIDTH
MAX_PAGES_PER_STEP = 16
SB_RECENT_PAGES = 4
MAX_REST_PAGES_PER_STEP = 16
DIFF_CHUNKS_PER_STEP = 4

NEG_BIG = -1e30
SB_DONE_LOG = -110.0

_NT = (((1,), (1,)), ((), ()))


def _dot(a, b):
    return jnp.dot(a, b, preferred_element_type=F32)


def _dot_nt(a, b):
    return lax.dot_general(a, b, _NT, preferred_element_type=F32)


def _split(x):
    hi = x.astype(BF16)
    return hi, (x - hi.astype(F32)).astype(BF16)


def _split_dot(x, m):
    hi, lo = _split(x)
    return _dot(hi, m) + _dot(lo, m)


def _rms_rows(x, g):
    ms = jnp.mean(x * x, axis=-1, keepdims=True)
    return x * lax.rsqrt(ms + EPS) * g


def _tile_lanes(x, width):
    reps = width // x.shape[-1]
    return x if reps == 1 else jnp.concatenate([x] * reps, axis=-1)


def _cparams(sem, limit=VMEM_LIMIT):
    return pltpu.CompilerParams(dimension_semantics=sem, vmem_limit_bytes=limit)


def _const_spec(shape):
    nd = len(shape)
    return pl.BlockSpec(shape, lambda *_: (0,) * nd, pipeline_mode=pl.Buffered(1))


def _in_proj_kernel(x_ref, g_ref, wrow_ref, wt_ref, cos_ref, sin_ref, cost_ref, sint_ref,
                    seg64_ref, seg128_ref, dqn_ref, dkn_ref, mqn_ref,
                    sbq_b, dq_b, dv_f, dv_b, mq_b, sbkt_f, sbkt_b, sbvt_f, sbvt_b, dkt_f, dkt_b):
    h = _rms_rows(x_ref[0], g_ref[...]).astype(BF16)
    tm = h.shape[0]
    wd = BRANCH_WIDTH

    def proj(s):
        return _dot(h, wrow_ref[:, s * wd:(s + 1) * wd])

    def proj_t(s):
        return _dot_nt(wt_ref[s * wd:(s + 1) * wd, :], h)

    def put_t(f_ref, b_ref, yt):
        f_ref[0] = yt
        for c in range(tm // KV_CHUNK):
            b_ref[0, c] = yt[:, c * KV_CHUNK:(c + 1) * KV_CHUNK].astype(BF16)

    scale64 = HEAD64 ** -0.5
    half = HEAD64 // 2

    sbq_b[0] = (proj(0) * scale64).astype(BF16)

    y = proj(1)
    y = y * lax.rsqrt(_split_dot(y * y, seg64_ref[...]) + EPS) * dqn_ref[...]
    lane = lax.broadcasted_iota(jnp.int32, y.shape, 1)
    swapped = jnp.where((lane % HEAD64) < half, pltpu.roll(y, wd - half, 1), pltpu.roll(y, half, 1))
    y = y * _tile_lanes(cos_ref[...], wd) + swapped * _tile_lanes(sin_ref[...], wd)
    dq_b[0] = (y * scale64).astype(BF16)

    y = proj(2)
    dv_f[0] = y
    dv_b[0] = y.astype(BF16)

    y = proj(3)
    y = y * lax.rsqrt(_split_dot(y * y, seg128_ref[...]) + EPS) * mqn_ref[...]
    mq_b[0] = y.astype(BF16)

    put_t(sbkt_f, sbkt_b, proj_t(0))
    put_t(sbvt_f, sbvt_b, proj_t(1))

    yt = proj_t(2)
    hi, lo = _split(yt * yt)
    ms = _dot(seg64_ref[...], hi) + _dot(seg64_ref[...], lo)
    yt = yt * lax.rsqrt(ms + EPS) * _tile_lanes(dkn_ref[...], tm)
    parts = []
    for grp in range(wd // HEAD64):
        lo_rows = yt[grp * HEAD64:grp * HEAD64 + half]
        hi_rows = yt[grp * HEAD64 + half:(grp + 1) * HEAD64]
        parts += [hi_rows, lo_rows]
    swapped_t = jnp.concatenate(parts, axis=0)
    reps = wd // HEAD64
    cos_t = jnp.concatenate([cost_ref[...]] * reps, axis=0)
    sin_t = jnp.concatenate([sint_ref[...]] * reps, axis=0)
    put_t(dkt_f, dkt_b, yt * cos_t + swapped_t * sin_t)


def _in_proj(x3d, g, w_row, w_t, cos_tab, sin_tab, cos_t, sin_t, seg64, seg128, dqn, dkn_col, mqn, tm):
    b, t, d = x3d.shape
    wd = BRANCH_WIDTH
    nt = t // tm
    nc = tm // KV_CHUNK
    row_spec = pl.BlockSpec((1, tm, wd), lambda bi, ti: (bi, ti, 0))
    t_spec = pl.BlockSpec((1, wd, tm), lambda bi, ti: (bi, 0, ti))
    tb_spec = pl.BlockSpec((1, nc, wd, KV_CHUNK), lambda bi, ti: (bi, ti, 0, 0))
    row = lambda dt: jax.ShapeDtypeStruct((b, t, wd), dt)
    tr_f = jax.ShapeDtypeStruct((b, wd, t), F32)
    tr_b = jax.ShapeDtypeStruct((b, t // KV_CHUNK, wd, KV_CHUNK), BF16)
    return pl.pallas_call(
        _in_proj_kernel,
        out_shape=(row(BF16), row(BF16), row(F32), row(BF16), row(BF16),
                   tr_f, tr_b, tr_f, tr_b, tr_f, tr_b),
        grid=(b, nt),
        in_specs=[
            pl.BlockSpec((1, tm, d), lambda bi, ti: (bi, ti, 0)),
            _const_spec(g.shape),
            _const_spec(w_row.shape),
            _const_spec(w_t.shape),
            pl.BlockSpec((tm, LANES), lambda bi, ti: (ti, 0)),
            pl.BlockSpec((tm, LANES), lambda bi, ti: (ti, 0)),
            pl.BlockSpec((HEAD64, tm), lambda bi, ti: (0, ti)),
            pl.BlockSpec((HEAD64, tm), lambda bi, ti: (0, ti)),
            _const_spec(seg64.shape),
            _const_spec(seg128.shape),
            _const_spec(dqn.shape),
            _const_spec(dkn_col.shape),
            _const_spec(mqn.shape),
        ],
        out_specs=(row_spec, row_spec, row_spec, row_spec, row_spec,
                   t_spec, tb_spec, t_spec, tb_spec, t_spec, tb_spec),
        compiler_params=_cparams(("parallel", "parallel")),
        name="in_proj",
    )(x3d, g, w_row, w_t, cos_tab, sin_tab, cos_t, sin_t, seg64, seg128, dqn, dkn_col, mqn)


def _mem_kv_kernel(x_ref, g_ref, w_ref, seg128_ref, kn_ref, mk_f, mk_b, mv_f, mv_b):
    h = _rms_rows(x_ref[...], g_ref[...]).astype(BF16)
    wd = BRANCH_WIDTH
    k = _dot(h, w_ref[:, :wd])
    k = k * lax.rsqrt(_split_dot(k * k, seg128_ref[...]) + EPS) * kn_ref[...]
    mk_f[...] = k
    mk_b[...] = k.astype(BF16)
    v = _dot(h, w_ref[:, wd:])
    mv_f[...] = v
    mv_b[...] = v.astype(BF16)


def _mem_kv(mem2d, g, w_kv, seg128, kn):
    m = mem2d.shape[0]
    wd = BRANCH_WIDTH
    full = lambda shape: pl.BlockSpec(shape, lambda i: (0, 0))
    return pl.pallas_call(
        _mem_kv_kernel,
        out_shape=tuple(jax.ShapeDtypeStruct((m, wd), dt) for dt in (F32, BF16, F32, BF16)),
        grid=(1,),
        in_specs=[full(mem2d.shape), full(g.shape), full(w_kv.shape), full(seg128.shape),
                  full(kn.shape)],
        out_specs=tuple(full((m, wd)) for _ in range(4)),
        compiler_params=_cparams(("arbitrary",)),
        name="mem_kv",
    )(mem2d, g, w_kv, seg128, kn)


def _sb_logs(z, vis):
    sp = jnp.log(1.0 + jnp.exp(-jnp.abs(z)))
    log_keep = -(jnp.maximum(z, 0.0) + sp)
    log_take = jnp.minimum(z, 0.0) - sp
    if vis is not None:
        log_keep = jnp.where(vis, log_keep, 0.0)
    return log_keep, log_take


def _sb_weights(log_take, excl, c, vis):
    w = jnp.exp(log_take + excl + _tile_lanes(c, excl.shape[-1]))
    return w if vis is None else jnp.where(vis, w, 0.0)


def _sb_prompt_kernel(q_ref, kt_ref, vt_ref, u_ref, o_ref, acc_ref, c_ref, *, tq):
    qi = pl.program_id(2)
    q = q_ref[0]
    lane = lax.broadcasted_iota(jnp.int32, (tq, LANES), 1)
    first = lane < HEAD64
    zero = jnp.zeros_like(q)
    q_heads = (jnp.where(first, q, zero), jnp.where(first, zero, q))
    acc_ref[...] = jnp.zeros_like(acc_ref)
    c_ref[...] = jnp.zeros_like(c_ref)
    tk = KV_CHUNK

    def tile(kc, masked):
        kt = kt_ref[0, kc]
        vt = vt_ref[0, kc]
        vis = None
        if masked:
            vis = (lax.broadcasted_iota(jnp.int32, (tq, tk), 1)
                   < lax.broadcasted_iota(jnp.int32, (tq, tk), 0))
        pv = []
        for hh in range(2):
            z = _dot(q_heads[hh], kt)
            log_keep, log_take = _sb_logs(z, vis)
            ee = _split_dot(log_keep, u_ref[...])
            c = c_ref[hh]
            w = _sb_weights(log_take, ee[:, :tk], c, vis)
            pv.append(_dot_nt(w.astype(BF16), vt))
            c_ref[hh] = c + ee[:, tk:]
        acc_ref[...] += jnp.where(first, pv[0], pv[1])

    tile(qi, True)

    def cond(state):
        kc, done = state
        return jnp.logical_and(kc >= 0, done == 0)

    def body(state):
        kc, _ = state
        tile(kc, False)
        done = (jnp.max(c_ref[...]) < SB_DONE_LOG).astype(jnp.int32)
        return kc - 1, done

    lax.while_loop(cond, body, (qi - 1, jnp.int32(0)))
    o_ref[0] = acc_ref[...].astype(o_ref.dtype)


def _reverse_cumsum_matrix(tk):
    j = jnp.arange(tk)
    u = (j[:, None] > j[None, :]).astype(BF16)
    return jnp.concatenate([u, jnp.ones((tk, LANES), BF16)], axis=1)


def _sb_prompt(q, kt, vt):
    b, t, wd = q.shape
    n_chunks = kt.shape[1]
    tq = KV_CHUNK
    kernel = functools.partial(_sb_prompt_kernel, tq=tq)
    qspec = pl.BlockSpec((1, tq, LANES), lambda bi, hp, qi: (bi, qi, hp))
    kvspec = pl.BlockSpec((1, n_chunks, LANES, KV_CHUNK), lambda bi, hp, qi: (bi, 0, hp, 0))
    return pl.pallas_call(
        kernel,
        out_shape=jax.ShapeDtypeStruct((b, t, wd), BF16),
        grid=(b, wd // LANES, t // tq),
        in_specs=[qspec, kvspec, kvspec, _const_spec((KV_CHUNK, KV_CHUNK + LANES))],
        out_specs=qspec,
        scratch_shapes=[pltpu.VMEM((tq, LANES), F32), pltpu.VMEM((2, tq, LANES), F32)],
        compiler_params=_cparams(("parallel", "parallel", "parallel")),
        name="sb_prompt",
    )(q, kt, vt, _reverse_cumsum_matrix(KV_CHUNK))


def _lambda_value(lam_ref, lam_init):
    lv = lam_ref[...]
    a = jnp.sum(lv[0:1] * lv[1:2], axis=-1, keepdims=True)
    b = jnp.sum(lv[2:3] * lv[3:4], axis=-1, keepdims=True)
    return jnp.exp(a) - jnp.exp(b) + lam_init


def _diff_prompt_kernel(lam_ref, sub_ref, q_ref, kt_ref, v_ref, o_ref, m_ref, acc_ref,
                        *, tq, lam_init):
    qi = pl.program_id(2)
    q = q_ref[0]
    lane = lax.broadcasted_iota(jnp.int32, (tq, LANES), 1)
    first = lane < HEAD64
    zero = jnp.zeros_like(q)
    qs = jnp.concatenate([jnp.where(first, q, zero), jnp.where(first, zero, q)], axis=0)
    m_ref[...] = jnp.full_like(m_ref, NEG_BIG)
    acc_ref[...] = jnp.zeros_like(acc_ref)
    ck = KV_CHUNK
    cps = DIFF_CHUNKS_PER_STEP

    def step(chunks, v, masked):
        width = ck * len(chunks)
        v_ext = jnp.concatenate([v, jnp.ones_like(v)], axis=1)
        s = [_dot(qs, kt_ref[0, c]) for c in chunks]
        s = s[0] if len(s) == 1 else jnp.concatenate(s, axis=1)
        if masked:
            row = lax.broadcasted_iota(jnp.int32, (2 * tq, width), 0) % tq
            col = lax.broadcasted_iota(jnp.int32, (2 * tq, width), 1) - (width - ck)
            s = jnp.where(col <= row, s, NEG_BIG)
        m_prev = m_ref[...]
        m_new = jnp.maximum(m_prev, jnp.max(s, axis=-1, keepdims=True))
        p = jnp.exp(s - _tile_lanes(m_new, width))
        alpha = jnp.exp(m_prev - m_new)
        acc_ref[...] = _tile_lanes(alpha, 2 * LANES) * acc_ref[...] + _dot(p.astype(BF16), v_ext)
        m_ref[...] = m_new

    def v_rows(chunk, n):
        return v_ref[0, pl.ds(pl.multiple_of(chunk * ck, ck), n * ck), :]

    def body(kt, carry):
        step([cps * kt + i for i in range(cps)], v_rows(cps * kt, cps), False)
        return carry

    n_full = qi // cps
    lax.fori_loop(0, n_full, body, 0)

    base = n_full * cps
    for extra in range(cps):
        @pl.when(qi - base == extra)
        def _():
            step([base + i for i in range(extra + 1)], v_rows(base, extra + 1), True)

    lam = _lambda_value(lam_ref, lam_init)
    a0, a1 = acc_ref[0:tq], acc_ref[tq:2 * tq]
    o = a0[:, :LANES] / a0[:, LANES:] - lam * (a1[:, :LANES] / a1[:, LANES:])
    o = _rms_rows(o, sub_ref[...]) * (1.0 - lam_init)
    o_ref[0] = o.astype(o_ref.dtype)


def _diff_prompt(lam_vecs, subln, q, kt, v, lam_init):
    b, t, wd = q.shape
    n_chunks = kt.shape[1]
    tq = KV_CHUNK
    kernel = functools.partial(_diff_prompt_kernel, tq=tq, lam_init=lam_init)
    qspec = pl.BlockSpec((1, tq, LANES), lambda bi, h, qi: (bi, qi, h))
    ktspec = pl.BlockSpec((1, n_chunks, LANES, KV_CHUNK), lambda bi, h, qi: (bi, 0, h, 0))
    vspec = pl.BlockSpec((1, t, LANES), lambda bi, h, qi: (bi, 0, h))
    return pl.pallas_call(
        kernel,
        out_shape=jax.ShapeDtypeStruct((b, t, wd), BF16),
        grid=(b, wd // LANES, t // tq),
        in_specs=[_const_spec(lam_vecs.shape), _const_spec(subln.shape), qspec, ktspec, vspec],
        out_specs=qspec,
        scratch_shapes=[pltpu.VMEM((2 * tq, LANES), F32), pltpu.VMEM((2 * tq, 2 * LANES), F32)],
        compiler_params=_cparams(("parallel", "parallel", "parallel")),
        name="diff_prompt",
    )(lam_vecs, subln, q, kt, v)


def _softmax_pv(s, v):
    p = jnp.exp(s - jnp.max(s, axis=-1, keepdims=True))
    return _dot(p.astype(BF16), v) / jnp.sum(p, axis=-1, keepdims=True)


def _mem_attend_prompt_kernel(q_ref, k_ref, v_ref, o_ref):
    q, k, v = q_ref[0], k_ref[0], v_ref[0]
    scale = HEAD128 ** -0.5
    outs = []
    for h in range(MEM_HEADS):
        sl = slice(h * HEAD128, (h + 1) * HEAD128)
        outs.append(_softmax_pv(_dot_nt(q[:, sl], k[:, sl]) * scale, v[:, sl]))
    o_ref[0] = jnp.concatenate(outs, axis=-1).astype(o_ref.dtype)


def _mem_attend_prompt(q, k, v, tq):
    g, r, wd = q.shape
    n_mem = k.shape[1]
    qspec = pl.BlockSpec((1, tq, wd), lambda gi, ri: (gi, ri, 0))
    kvspec = pl.BlockSpec((1, n_mem, wd), lambda gi, ri: (gi, 0, 0))
    return pl.pallas_call(
        _mem_attend_prompt_kernel,
        out_shape=jax.ShapeDtypeStruct((g, r, wd), BF16),
        grid=(g, r // tq),
        in_specs=[qspec, kvspec, kvspec],
        out_specs=qspec,
        compiler_params=_cparams(("parallel", "parallel")),
        name="mem_attend_prompt",
    )(q, k, v)


def _mem_attend_sample_kernel(q_ref, k_ref, v_ref, o_ref, *, n_mem):
    q = q_ref[0]
    scale = HEAD128 ** -0.5
    outs = []
    for h in range(MEM_HEADS):
        sl = slice(h * HEAD128, (h + 1) * HEAD128)
        k = k_ref[0, pl.ds(h, n_mem, stride=MEM_HEADS), :].astype(BF16)
        v = v_ref[0, pl.ds(h, n_mem, stride=MEM_HEADS), :].astype(BF16)
        outs.append(_softmax_pv(_dot_nt(q[:, sl], k) * scale, v))
    o_ref[0] = jnp.concatenate(outs, axis=-1).astype(o_ref.dtype)


def _mem_attend_sample(q, k_rows, v_rows, n_mem):
    db, r, wd = q.shape
    qspec = pl.BlockSpec((1, r, wd), lambda bi: (bi, 0, 0))
    kvspec = pl.BlockSpec((1, n_mem * MEM_HEADS, HEAD128), lambda bi: (bi, 0, 0))
    return pl.pallas_call(
        functools.partial(_mem_attend_sample_kernel, n_mem=n_mem),
        out_shape=jax.ShapeDtypeStruct((db, r, wd), BF16),
        grid=(db,),
        in_specs=[qspec, kvspec, kvspec],
        out_specs=qspec,
        compiler_params=_cparams(("parallel",)),
        name="mem_attend_sample",
    )(q, k_rows, v_rows)


def _sb_pages(q, k_refs, v_refs, vis, uext, acc, c):
    rows = q.shape[0]
    n = len(k_refs)
    logs = [_sb_logs(_dot(q, k[0].astype(BF16)), vis) for k in k_refs]
    log_keep = logs[0][0] if n == 1 else jnp.concatenate([lg[0] for lg in logs], axis=0)
    ee = _split_dot(log_keep, uext)
    for i in range(n):
        e = ee[i * rows:(i + 1) * rows]
        w = _sb_weights(logs[i][1], e[:, :PAGE_SIZE], c, vis)
        acc = acc + _dot_nt(w.astype(BF16), v_refs[i][0].astype(BF16))
        c = c + e[:, PAGE_SIZE:]
    return acc, c


def _sb_sample_recent_kernel(pt_ref, q_ref, newk_ref, newv_ref, *rest, n_recent):
    k_refs, v_refs = rest[:n_recent], rest[n_recent:2 * n_recent]
    u_ref, acc_out, c_out, done_out = rest[2 * n_recent:]
    q = q_ref[0]
    rows = q.shape[0]
    t = lax.broadcasted_iota(jnp.int32, (rows, PAGE_SIZE), 0) // SUBLANES
    s = lax.broadcasted_iota(jnp.int32, (rows, PAGE_SIZE), 1)
    acc = jnp.zeros(acc_out.shape[1:], F32)
    c = jnp.zeros(c_out.shape[1:], F32)
    acc, c = _sb_pages(q, [newk_ref], [newv_ref], s < t, u_ref[...], acc, c)
    acc, c = _sb_pages(q, k_refs, v_refs, None, u_ref[...], acc, c)
    acc_out[0] = acc
    c_out[0] = c
    c_max = jnp.max(jnp.max(c, axis=1, keepdims=True), axis=0, keepdims=True)
    done_out[0] = jnp.broadcast_to((c_max < SB_DONE_LOG).astype(jnp.int32), done_out.shape[1:])


def _sb_sample_rest_kernel(pt_ref, done_ref, q_ref, acc_in, c_in, *rest, n_new, g_pages):
    k_refs, v_refs = rest[:g_pages], rest[g_pages:2 * g_pages]
    u_ref, gmask_ref, o_ref, acc_ref, c_ref = rest[2 * g_pages:]
    b = pl.program_id(0)
    j = pl.program_id(1)

    @pl.when(j == 0)
    def _():
        acc_ref[...] = acc_in[0]
        c_ref[...] = c_in[0]

    @pl.when(done_ref[b] == 0)
    def _():
        @pl.when(jnp.max(c_ref[...]) >= SB_DONE_LOG)
        def _():
            acc, c = _sb_pages(q_ref[0], k_refs, v_refs, None, u_ref[...], acc_ref[...], c_ref[...])
            acc_ref[...] = acc
            c_ref[...] = c

    @pl.when(j == pl.num_programs(1) - 1)
    def _():
        masked = acc_ref[...] * gmask_ref[...]
        o = jnp.sum(masked.reshape(n_new, SUBLANES, masked.shape[-1]), axis=1)
        o_ref[0] = o.astype(o_ref.dtype)


def _diff_sample_kernel(pt_ref, lam_ref, sub_ref, q_ref, newk_ref, newv_ref, *rest,
                        n_new, g_pages, lam_init):
    k_refs, v_refs = rest[:g_pages], rest[g_pages:2 * g_pages]
    o_ref, m_ref, l_ref, acc_ref = rest[2 * g_pages:]
    j = pl.program_id(1)
    q = q_ref[0]
    rows = q.shape[0]

    def pages(ks, vs, vis):
        n = len(ks)
        s = [_dot(q, k[0].astype(BF16)) for k in ks]
        s = s[0] if n == 1 else jnp.concatenate(s, axis=1)
        if vis is not None:
            s = jnp.where(vis, s, NEG_BIG)
        m_prev = m_ref[...]
        m_new = jnp.maximum(m_prev, jnp.max(s, axis=-1, keepdims=True))
        p = jnp.exp(s - _tile_lanes(m_new, n * PAGE_SIZE))
        alpha = jnp.exp(m_prev - m_new)
        l_ref[...] = alpha * l_ref[...] + jnp.sum(p, axis=-1, keepdims=True)
        pv = []
        for h in range(DIFF_HEADS):
            ph = p[h * SUBLANES:(h + 1) * SUBLANES].astype(BF16)
            tot = None
            for i in range(n):
                vh = vs[i][0, pl.ds(h, PAGE_SIZE, stride=DIFF_HEADS), :].astype(BF16)
                term = _dot(ph[:, i * PAGE_SIZE:(i + 1) * PAGE_SIZE], vh)
                tot = term if tot is None else tot + term
            pv.append(tot)
        acc_ref[...] = alpha * acc_ref[...] + jnp.concatenate(pv, axis=0)
        m_ref[...] = m_new

    @pl.when(j == 0)
    def _():
        m_ref[...] = jnp.full_like(m_ref, NEG_BIG)
        l_ref[...] = jnp.zeros_like(l_ref)
        acc_ref[...] = jnp.zeros_like(acc_ref)
        t = (lax.broadcasted_iota(jnp.int32, (rows, PAGE_SIZE), 0) % SUBLANES) // 2
        s = lax.broadcasted_iota(jnp.int32, (rows, PAGE_SIZE), 1)
        pages([newk_ref], [newv_ref], s <= t)

    pages(k_refs, v_refs, None)

    @pl.when(j == pl.num_programs(1) - 1)
    def _():
        lam = _lambda_value(lam_ref, lam_init)
        r = lax.broadcasted_iota(jnp.int32, (rows, LANES), 0)
        coef = jnp.where(r % 2 == 0, 1.0, -lam)
        acc_ref[...] = acc_ref[...] * (coef / l_ref[...])
        outs = []
        for h in range(DIFF_HEADS):
            o = (acc_ref[pl.ds(h * SUBLANES, n_new, stride=2), :]
                 + acc_ref[pl.ds(h * SUBLANES + 1, n_new, stride=2), :])
            outs.append(_rms_rows(o, sub_ref[...]) * (1.0 - lam_init))
        o_ref[0] = jnp.concatenate(outs, axis=-1).astype(o_ref.dtype)


def _pages_per_step(n_pages):
    return max(g for g in range(1, MAX_PAGES_PER_STEP + 1) if n_pages % g == 0)


def _paged_specs(n_pages, g_pages):
    per_b = lambda shape: pl.BlockSpec(shape, lambda b, j, pt: (b, 0, 0))
    const = lambda shape: pl.BlockSpec(shape, lambda b, j, pt: (0,) * len(shape))

    def page(g):
        return pl.BlockSpec((1, BRANCH_WIDTH, PAGE_SIZE),
                            lambda b, j, pt: (pt[b, n_pages - 1 - (j * g_pages + g)], 0, 0))

    return per_b, const, [page(g) for g in range(g_pages)]


def _sb_sample(page_table, qbd, newk, newv, pool_k, pool_v, gmask, n_new):
    db, rows, wd = qbd.shape
    n_pages = page_table.shape[1]
    assert n_pages >= 2
    n_recent = min(SB_RECENT_PAGES, n_pages // 2)
    n_rest = n_pages - n_recent
    g_pages = max(g for g in range(1, MAX_REST_PAGES_PER_STEP + 1) if n_rest % g == 0)
    uext = _reverse_cumsum_matrix(PAGE_SIZE)
    page_block = (1, wd, PAGE_SIZE)

    per_b = lambda shape: pl.BlockSpec(shape, lambda b, pt: (b, 0, 0))
    recent = [pl.BlockSpec(page_block, lambda b, pt, g=g: (pt[b, n_pages - 1 - g], 0, 0))
              for g in range(n_recent)]
    acc, c, done = pl.pallas_call(
        functools.partial(_sb_sample_recent_kernel, n_recent=n_recent),
        out_shape=(jax.ShapeDtypeStruct((db, rows, wd), F32),
                   jax.ShapeDtypeStruct((db, rows, LANES), F32),
                   jax.ShapeDtypeStruct((db, SUBLANES, LANES), jnp.int32)),
        grid_spec=pltpu.PrefetchScalarGridSpec(
            num_scalar_prefetch=1,
            grid=(db,),
            in_specs=[per_b((1, rows, wd)), per_b(page_block), per_b(page_block)]
                     + recent + recent + [pl.BlockSpec(uext.shape, lambda b, pt: (0, 0))],
            out_specs=(per_b((1, rows, wd)), per_b((1, rows, LANES)), per_b((1, SUBLANES, LANES))),
        ),
        compiler_params=_cparams(("parallel",)),
        name="sb_sample_recent",
    )(page_table, qbd, newk, newv, *([pool_k] * n_recent), *([pool_v] * n_recent), uext)
    done = done[:, 0, 0]

    per_b2 = lambda shape: pl.BlockSpec(shape, lambda b, j, pt, dn: (b, 0, 0))
    const2 = lambda shape: pl.BlockSpec(shape, lambda b, j, pt, dn: (0,) * len(shape))

    def older(g):
        def index(b, j, pt, dn):
            page = pt[b, n_rest - 1 - (j * g_pages + g)]
            return (jnp.where(dn[b] == 0, page, 0), 0, 0)
        return pl.BlockSpec(page_block, index)

    older_specs = [older(g) for g in range(g_pages)]
    return pl.pallas_call(
        functools.partial(_sb_sample_rest_kernel, n_new=n_new, g_pages=g_pages),
        out_shape=jax.ShapeDtypeStruct((db, n_new, wd), F32),
        grid_spec=pltpu.PrefetchScalarGridSpec(
            num_scalar_prefetch=2,
            grid=(db, n_rest // g_pages),
            in_specs=[per_b2((1, rows, wd)), per_b2((1, rows, wd)), per_b2((1, rows, LANES))]
                     + older_specs + older_specs + [const2(uext.shape), const2(gmask.shape)],
            out_specs=per_b2((1, n_new, wd)),
            scratch_shapes=[pltpu.VMEM((rows, wd), F32), pltpu.VMEM((rows, LANES), F32)],
        ),
        compiler_params=_cparams(("parallel", "arbitrary")),
        name="sb_sample_rest",
    )(page_table, done, qbd, acc, c, *([pool_k] * g_pages), *([pool_v] * g_pages), uext, gmask)


def _diff_sample(page_table, lam_vecs, subln, qbd, newk, newv, pool_k, pool_v, n_new, lam_init):
    db, rows, wd = qbd.shape
    n_pages = page_table.shape[1]
    g_pages = _pages_per_step(n_pages)
    per_b, const, page_specs = _paged_specs(n_pages, g_pages)
    grid_spec = pltpu.PrefetchScalarGridSpec(
        num_scalar_prefetch=1,
        grid=(db, n_pages // g_pages),
        in_specs=[const(lam_vecs.shape), const(subln.shape),
                  per_b((1, rows, wd)), per_b((1, wd, PAGE_SIZE)), per_b((1, wd, PAGE_SIZE))]
                 + page_specs + page_specs,
        out_specs=per_b((1, n_new, wd)),
        scratch_shapes=[pltpu.VMEM((rows, LANES), F32), pltpu.VMEM((rows, LANES), F32),
                        pltpu.VMEM((rows, LANES), F32)],
    )
    return pl.pallas_call(
        functools.partial(_diff_sample_kernel, n_new=n_new, g_pages=g_pages, lam_init=lam_init),
        out_shape=jax.ShapeDtypeStruct((db, n_new, wd), F32),
        grid_spec=grid_spec,
        compiler_params=_cparams(("parallel", "arbitrary")),
        name="diff_sample",
    )(page_table, lam_vecs, subln, qbd, newk, newv, *([pool_k] * g_pages), *([pool_v] * g_pages))


def _merge_kernel(x_ref, osb_ref, odf_ref, omem_ref, g_ref, wg_ref, bg_ref, wb_ref, wo_ref,
                  o_ref):
    x = x_ref[...]
    h = _rms_rows(x, g_ref[...]).astype(BF16)
    merged = None
    for b, ob_ref in enumerate((osb_ref, odf_ref, omem_ref)):
        gate = _dot(h, wg_ref[:, b * D_MODEL:(b + 1) * D_MODEL]) + bg_ref[b:b + 1, :]
        term = jax.nn.sigmoid(gate) * _dot(ob_ref[...].astype(BF16), wb_ref[b])
        merged = term if merged is None else merged + term
    o_ref[...] = x + _dot(merged.astype(BF16), wo_ref[...])


def _merge(x2d, o_sb, o_diff, o_mem, g, w_gate, b_gate, w_branch, w_out, tm):
    m = x2d.shape[0]
    row = lambda i: (i, 0)
    wide = pl.BlockSpec((tm, D_MODEL), row)
    narrow = pl.BlockSpec((tm, BRANCH_WIDTH), row)
    return pl.pallas_call(
        _merge_kernel,
        out_shape=jax.ShapeDtypeStruct((m, D_MODEL), F32),
        grid=(m // tm,),
        in_specs=[wide, narrow, narrow, narrow, _const_spec(g.shape), _const_spec(w_gate.shape),
                  _const_spec(b_gate.shape), _const_spec(w_branch.shape),
                  _const_spec(w_out.shape)],
        out_specs=wide,
        compiler_params=_cparams(("parallel",)),
        name="merge",
    )(x2d, o_sb, o_diff, o_mem, g, w_gate, b_gate, w_branch, w_out)


def _ffn_chunks(d_ff):
    chunk = MXU_WIDTH if d_ff % MXU_WIDTH == 0 else LANES
    return chunk, d_ff // chunk


def _ffn_prompt_kernel(x_ref, g_ref, wup_ref, cw_ref, cb_ref, wdn_ref, o_ref, tail_ref, abuf_ref,
                       *, tm, d_ff):
    pad = SUBLANES

    @pl.when(pl.program_id(1) == 0)
    def _():
        abuf_ref[0:pad, :] = jnp.zeros((pad, d_ff), F32)

    x = x_ref[0]
    h = _rms_rows(x, g_ref[...]).astype(BF16)
    chunk, n_chunks = _ffn_chunks(d_ff)
    acc = x
    for c in range(n_chunks):
        sl = slice(c * chunk, (c + 1) * chunk)
        abuf_ref[pad:pad + tm, sl] = _dot(h, wup_ref[:, sl])
        conv = (cb_ref[:, sl]
                + cw_ref[0:1, sl] * abuf_ref[pad - 2:pad - 2 + tm, sl]
                + cw_ref[1:2, sl] * abuf_ref[pad - 1:pad - 1 + tm, sl]
                + cw_ref[2:3, sl] * abuf_ref[pad:pad + tm, sl])
        up = _dot(h, wup_ref[:, d_ff + c * chunk:d_ff + (c + 1) * chunk])
        acc = acc + _dot((jax.nn.silu(conv) * up).astype(BF16), wdn_ref[sl, :])
    o_ref[0] = acc
    last = abuf_ref[tm:tm + pad, :]
    tail_ref[0] = last
    abuf_ref[0:pad, :] = last


def _ffn_prompt(x3d, g, w_up, conv_w, conv_b, w_down, tm):
    b, t, d = x3d.shape
    d_ff = w_down.shape[0]
    kernel = functools.partial(_ffn_prompt_kernel, tm=tm, d_ff=d_ff)
    xspec = pl.BlockSpec((1, tm, d), lambda bi, ti: (bi, ti, 0))
    return pl.pallas_call(
        kernel,
        out_shape=(jax.ShapeDtypeStruct((b, t, d), F32),
                   jax.ShapeDtypeStruct((b, SUBLANES, d_ff), F32)),
        grid=(b, t // tm),
        in_specs=[xspec, _const_spec(g.shape), _const_spec(w_up.shape), _const_spec(conv_w.shape),
                  _const_spec(conv_b.shape), _const_spec(w_down.shape)],
        out_specs=(xspec, pl.BlockSpec((1, SUBLANES, d_ff), lambda bi, ti: (bi, 0, 0))),
        scratch_shapes=[pltpu.VMEM((tm + SUBLANES, d_ff), F32)],
        compiler_params=_cparams(("parallel", "arbitrary")),
        name="ffn_prompt",
    )(x3d, g, w_up, conv_w, conv_b, w_down)


def _ffn_sample_kernel(x_ref, g_ref, pre_ref, wa_ref, wu_ref, cw_ref, cb_ref, wdn_ref,
                       o_ref, a_ref, h_ref, *, n_new, db):
    c = pl.program_id(0)

    @pl.when(c == 0)
    def _():
        x = x_ref[...]
        h_ref[...] = _rms_rows(x, g_ref[...]).astype(BF16)
        o_ref[...] = x

    h = h_ref[...]
    a = _dot(h, wa_ref[...])
    a_ref[...] = a
    up = _dot(h, wu_ref[...])
    slabs = [pre_ref[0], pre_ref[1]] + [a[t * db:(t + 1) * db] for t in range(n_new)]
    conv = jnp.concatenate(
        [cb_ref[...] + cw_ref[0:1] * slabs[t] + cw_ref[1:2] * slabs[t + 1] + cw_ref[2:3] * slabs[t + 2]
         for t in range(n_new)], axis=0)
    o_ref[...] += _dot((jax.nn.silu(conv) * up).astype(BF16), wdn_ref[...])


def _ffn_sample(x_tm, g, prefix_tm, w_up, conv_w, conv_b, w_down, n_new, db):
    m, d = x_tm.shape
    d_ff = w_down.shape[0]
    chunk, n_chunks = _ffn_chunks(d_ff)
    kernel = functools.partial(_ffn_sample_kernel, n_new=n_new, db=db)
    fixed = lambda shape: pl.BlockSpec(shape, lambda c: (0,) * len(shape))
    return pl.pallas_call(
        kernel,
        out_shape=(jax.ShapeDtypeStruct((m, d), F32), jax.ShapeDtypeStruct((m, d_ff), F32)),
        grid=(n_chunks,),
        in_specs=[fixed((m, d)), fixed(g.shape),
                  pl.BlockSpec((CONV_WIDTH - 1, db, chunk), lambda c: (0, 0, c)),
                  pl.BlockSpec((d, chunk), lambda c: (0, c)),
                  pl.BlockSpec((d, chunk), lambda c: (0, n_chunks + c)),
                  pl.BlockSpec((CONV_WIDTH, chunk), lambda c: (0, c)),
                  pl.BlockSpec((1, chunk), lambda c: (0, c)),
                  pl.BlockSpec((chunk, d), lambda c: (c, 0))],
        out_specs=(fixed((m, d)), pl.BlockSpec((m, chunk), lambda c: (0, c))),
        scratch_shapes=[pltpu.VMEM((m, d), BF16)],
        compiler_params=_cparams(("arbitrary",)),
        name="ffn_sample",
    )(x_tm, g, prefix_tm, w_up, w_up, conv_w, conv_b, w_down)


def _rope_tables(pos):
    half = HEAD64 // 2
    inv_freq = jnp.power(ROPE_THETA, -jnp.arange(half, dtype=F32) / half)
    ang = pos.astype(F32)[:, None] * inv_freq[None, :]
    cos, sin = jnp.cos(ang), jnp.sin(ang)
    cos = jnp.concatenate([cos, cos], axis=-1)
    sin = jnp.concatenate([-sin, sin], axis=-1)
    reps = LANES // HEAD64
    return jnp.tile(cos, (1, reps)), jnp.tile(sin, (1, reps)), cos.T, sin.T


def _seg_matrix(width, seg):
    i = jnp.arange(width) // seg
    return ((i[:, None] == i[None, :]).astype(F32) / seg).astype(BF16)


def _tile_gain(gain, width):
    return jnp.tile(gain.astype(F32), width // gain.shape[0]).reshape(1, width)


def _block_diag_queries(q, t_idx, grp):
    lane_grp = jnp.arange(q.shape[-1]) // HEAD64
    mask = (lane_grp[None, :] == grp[:, None]).astype(q.dtype)
    return jnp.take(q, t_idx, axis=1) * mask[None]


def _state_from_t(yt, lead, heads):
    n = len(heads)
    y = yt.reshape(tuple(heads) + tuple(lead))
    perm = tuple(range(n, n + len(lead))) + tuple(range(n))
    return jnp.transpose(y, perm)


def kernel(x_prompt, x_sample, cache_sb_k, cache_sb_v, cache_diff_k, cache_diff_v, cache_mem_k, cache_mem_v, state_conv, page_table, mem_prompt, norm_mix, norm_mem, w_in, b_gate, diff_q_norm, diff_k_norm, lambda_q1, lambda_k1, lambda_q2, lambda_k2, diff_subln, w_mem_kv, mem_q_norm, mem_k_norm, w_branch, w_out, norm_ffn, w_ffn_up, conv_w, conv_b, w_ffn_down):
    depth = w_in.shape[0]
    assert depth == 1, "single-layer step"
    l = 0
    lam_init = 0.8 - 0.6 * math.exp(-0.3 * l)
    bsz, seq, d = x_prompt.shape
    db, n_new, _ = x_sample.shape
    assert 2 * n_new == SUBLANES, "sample query rows are packed eight (t, map) rows per head"
    n_pages = page_table.shape[1]
    n_pool = cache_sb_k.shape[1]
    past_len = n_pages * PAGE_SIZE
    n_mem = mem_prompt.shape[1]
    wd = BRANCH_WIDTH
    d_ff = w_ffn_down.shape[1]
    n_attn = N_ATTN_SECTIONS * wd
    ms = db * n_new

    w_in_b = w_in[l].astype(BF16)
    sec = lambda s: w_in_b[:, s * wd:(s + 1) * wd]
    w_row = jnp.concatenate([sec(SEC_SB_Q), sec(SEC_D_Q), sec(SEC_D_V), sec(SEC_M_Q)], axis=1)
    w_t = jnp.concatenate([sec(SEC_SB_K), sec(SEC_SB_V), sec(SEC_D_K)], axis=1).T
    w_gate = w_in_b[:, n_attn:]
    w_branch_b = w_branch[l].astype(BF16)
    w_out_b = w_out[l].astype(BF16)
    w_up_b = w_ffn_up[l].astype(BF16)
    w_down_b = w_ffn_down[l].astype(BF16)
    w_kv_b = w_mem_kv[l].astype(BF16)
    g_mix = norm_mix[l].reshape(1, d)
    g_mem = norm_mem[l].reshape(1, d)
    g_ffn = norm_ffn[l].reshape(1, d)
    seg64, seg128 = _seg_matrix(wd, HEAD64), _seg_matrix(wd, HEAD128)
    dqn = _tile_gain(diff_q_norm[l], wd)
    dkn_col = jnp.broadcast_to(_tile_gain(diff_k_norm[l], wd).reshape(wd, 1), (wd, LANES))
    mqn, mkn = _tile_gain(mem_q_norm[l], wd), _tile_gain(mem_k_norm[l], wd)
    subln = diff_subln[l].reshape(1, HEAD128)
    lam_vecs = jnp.stack([lambda_q1[l], lambda_k1[l], lambda_q2[l], lambda_k2[l]]).astype(F32)
    cw, cb = conv_w[l], conv_b[l].reshape(1, d_ff)

    def in_proj(x3d, pos, tm):
        cos_r, sin_r, cos_c, sin_c = _rope_tables(pos)
        return _in_proj(x3d, g_mix, w_row, w_t, cos_r, sin_r, cos_c, sin_c, seg64, seg128,
                        dqn, dkn_col, mqn, tm)

    tm = min(512, seq)
    (sbq_b, dq_b, dv_f, dv_b, mq_b, sbkt_f, sbkt_b, sbvt_f, sbvt_b, dkt_f, dkt_b) = in_proj(
        x_prompt, jnp.arange(seq), tm)
    mk_f, mk_b, mv_f, mv_b = _mem_kv(mem_prompt.reshape(bsz * n_mem, d), g_mem, w_kv_b, seg128, mkn)

    o_sb = _sb_prompt(sbq_b, sbkt_b, sbvt_b)
    o_diff = _diff_prompt(lam_vecs, subln, dq_b, dkt_b, dv_b, lam_init)
    o_mem = _mem_attend_prompt(mq_b, mk_b.reshape(bsz, n_mem, wd), mv_b.reshape(bsz, n_mem, wd), tm)
    x1 = _merge(x_prompt.reshape(bsz * seq, d), o_sb.reshape(-1, wd), o_diff.reshape(-1, wd),
                o_mem.reshape(-1, wd), g_mix, w_gate, b_gate[l], w_branch_b, w_out_b, tm)
    y_prompt, conv_tail = _ffn_prompt(x1.reshape(bsz, seq, d), g_ffn, w_up_b, cw, cb, w_down_b, tm)

    pos_s = jnp.tile(past_len + jnp.arange(n_new), db)
    (ssbq_b, sdq_b, sdv_f, _, smq_b, ssbkt_f, _, ssbvt_f, _, sdkt_f, _) = in_proj(
        x_sample.reshape(1, ms, d), pos_s, ms)

    s3 = lambda a: a.reshape(db, n_new, wd)

    def new_t_page(yt):
        y = yt.reshape(wd, db, n_new).transpose(1, 0, 2)
        return jnp.pad(y, ((0, 0), (0, 0), (0, PAGE_SIZE - n_new)))

    r = jnp.arange(n_new * SUBLANES)
    t_sb, grp_sb = r // SUBLANES, r % SUBLANES
    t_df, grp_df = (r % SUBLANES) // 2, 2 * (r // SUBLANES) + r % 2
    gmask_sb = ((jnp.arange(wd) // HEAD64)[None, :] == grp_sb[:, None]).astype(F32)

    t_pages = lambda c: jnp.swapaxes(c[l].reshape(n_pool, PAGE_SIZE, wd), 1, 2)
    row_pages = lambda c: c[l].reshape(c.shape[1], -1, HEAD128)

    so_sb = _sb_sample(page_table, _block_diag_queries(s3(ssbq_b), t_sb, grp_sb),
                       new_t_page(ssbkt_f), new_t_page(ssbvt_f),
                       t_pages(cache_sb_k), t_pages(cache_sb_v), gmask_sb, n_new)
    new_dv = jnp.pad(sdv_f.reshape(db, n_new * DIFF_HEADS, HEAD128),
                     ((0, 0), (0, (PAGE_SIZE - n_new) * DIFF_HEADS), (0, 0)))
    so_diff = _diff_sample(page_table, lam_vecs, subln, _block_diag_queries(s3(sdq_b), t_df, grp_df),
                           new_t_page(sdkt_f), new_dv,
                           t_pages(cache_diff_k), row_pages(cache_diff_v), n_new, lam_init)
    q_rows = 2 * SUBLANES
    q_pad = jnp.pad(s3(smq_b), ((0, 0), (0, q_rows - n_new), (0, 0)))
    so_mem = _mem_attend_sample(q_pad, row_pages(cache_mem_k), row_pages(cache_mem_v), n_mem)[:, :n_new]
    xs1 = _merge(x_sample.reshape(ms, d), so_sb.reshape(ms, wd), so_diff.reshape(ms, wd),
                 so_mem.reshape(ms, wd), g_mix, w_gate, b_gate[l], w_branch_b, w_out_b, ms)
    xs1_tm = xs1.reshape(db, n_new, d).transpose(1, 0, 2).reshape(ms, d)
    prefix_tm = state_conv[l].transpose(1, 0, 2)
    ys_tm, a_tm = _ffn_sample(xs1_tm, g_ffn, prefix_tm, w_up_b, cw, cb, w_down_b, n_new, db)
    y_sample = ys_tm.reshape(n_new, db, d).transpose(1, 0, 2)
    a_ext = jnp.concatenate([prefix_tm, a_tm.reshape(n_new, db, d_ff)], axis=0)
    conv_s = a_ext[n_new:].transpose(1, 0, 2)

    st = lambda a, *shape: a.reshape((1,) + shape)
    sb_heads, df_heads = (SB_HEADS, HEAD64), (DIFF_HEADS, 2, HEAD64)
    return (
        y_prompt, y_sample,
        jnp.transpose(sbkt_f.reshape(bsz, SB_HEADS, HEAD64, seq), (0, 3, 1, 2))[None],
        jnp.transpose(sbvt_f.reshape(bsz, SB_HEADS, HEAD64, seq), (0, 3, 1, 2))[None],
        jnp.transpose(dkt_f.reshape(bsz, DIFF_HEADS, 2, HEAD64, seq), (0, 4, 1, 2, 3))[None],
        st(dv_f, bsz, seq, DIFF_HEADS, HEAD128),
        st(mk_f, bsz, n_mem, MEM_HEADS, HEAD128), st(mv_f, bsz, n_mem, MEM_HEADS, HEAD128),
        st(conv_tail[:, SUBLANES - (CONV_WIDTH - 1):], bsz, CONV_WIDTH - 1, d_ff),
        _state_from_t(ssbkt_f, (db, n_new), sb_heads)[None],
        _state_from_t(ssbvt_f, (db, n_new), sb_heads)[None],
        _state_from_t(sdkt_f, (db, n_new), df_heads)[None],
        st(sdv_f, db, n_new, DIFF_HEADS, HEAD128),
        st(conv_s, db, CONV_WIDTH - 1, d_ff),
    )
```

```python
import functools
import math

import jax
import jax.numpy as jnp
from jax import lax
from jax.experimental import pallas as pl
from jax.experimental.pallas import tpu as pltpu

F32 = jnp.float32
BF16 = jnp.bfloat16

D_MODEL = 1024
SB_HEADS = 8
DIFF_HEADS = 4
MEM_HEADS = 4
HEAD64 = 64
HEAD128 = 128
BRANCH_WIDTH = 512
N_BRANCH = 3
PAGE_SIZE = 128
CONV_WIDTH = 3
ROPE_THETA = 10000.0
EPS = 1e-6
SEC_SB_Q, SEC_SB_K, SEC_SB_V, SEC_D_Q, SEC_D_K, SEC_D_V, SEC_M_Q = range(7)
N_ATTN_SECTIONS = 7

LANES = 128
SUBLANES = 8
MXU_WIDTH = 256
VMEM_LIMIT = 56 * 1024 * 1024
KV_CHUNK = MXU_WIDTH
MAX_PAGES_PER_STEP = 16
SB_RECENT_PAGES = 4
MAX_REST_PAGES_PER_STEP = 16
DIFF_CHUNKS_PER_STEP = 4
DIFF_Q_CHUNKS = 2

NEG_BIG = -1e30
SB_DONE_LOG = -110.0

_NT = (((1,), (1,)), ((), ()))


def _dot(a, b):
    return jnp.dot(a, b, preferred_element_type=F32)


def _dot_nt(a, b):
    return lax.dot_general(a, b, _NT, preferred_element_type=F32)


def _split(x):
    hi = x.astype(BF16)
    return hi, (x - hi.astype(F32)).astype(BF16)


def _split_dot(x, m):
    hi, lo = _split(x)
    return _dot(hi, m) + _dot(lo, m)


def _rms_rows(x, g):
    ms = jnp.mean(x * x, axis=-1, keepdims=True)
    return x * lax.rsqrt(ms + EPS) * g


def _tile_lanes(x, width):
    reps = width // x.shape[-1]
    return x if reps == 1 else jnp.concatenate([x] * reps, axis=-1)


def _cparams(sem, limit=VMEM_LIMIT):
    return pltpu.CompilerParams(dimension_semantics=sem, vmem_limit_bytes=limit)


def _const_spec(shape):
    nd = len(shape)
    return pl.BlockSpec(shape, lambda *_: (0,) * nd, pipeline_mode=pl.Buffered(1))


def _in_proj_kernel(x_ref, g_ref, wrow_ref, wt_ref, cos_ref, sin_ref, cost_ref, sint_ref,
                    seg64_ref, seg128_ref, dqn_ref, dkn_ref, mqn_ref,
                    sbq_b, dq_b, dv_f, dv_b, mq_b, sbkt_f, sbkt_b, sbvt_f, sbvt_b, dkt_f, dkt_b):
    h = _rms_rows(x_ref[0], g_ref[...]).astype(BF16)
    tm = h.shape[0]
    wd = BRANCH_WIDTH

    def proj(s):
        return _dot(h, wrow_ref[:, s * wd:(s + 1) * wd])

    def proj_t(s):
        return _dot_nt(wt_ref[s * wd:(s + 1) * wd, :], h)

    def put_t(f_ref, b_ref, yt):
        f_ref[0] = yt
        for c in range(tm // KV_CHUNK):
            b_ref[0, c] = yt[:, c * KV_CHUNK:(c + 1) * KV_CHUNK].astype(BF16)

    scale64 = HEAD64 ** -0.5
    half = HEAD64 // 2

    sbq_b[0] = (proj(0) * scale64).astype(BF16)

    y = proj(1)
    y = y * lax.rsqrt(_split_dot(y * y, seg64_ref[...]) + EPS) * dqn_ref[...]
    lane = lax.broadcasted_iota(jnp.int32, y.shape, 1)
    swapped = jnp.where((lane % HEAD64) < half, pltpu.roll(y, wd - half, 1), pltpu.roll(y, half, 1))
    y = y * _tile_lanes(cos_ref[...], wd) + swapped * _tile_lanes(sin_ref[...], wd)
    dq_b[0] = (y * scale64).astype(BF16)

    y = proj(2)
    dv_f[0] = y
    dv_b[0] = y.astype(BF16)

    y = proj(3)
    y = y * lax.rsqrt(_split_dot(y * y, seg128_ref[...]) + EPS) * mqn_ref[...]
    mq_b[0] = y.astype(BF16)

    put_t(sbkt_f, sbkt_b, proj_t(0))
    put_t(sbvt_f, sbvt_b, proj_t(1))

    yt = proj_t(2)
    hi, lo = _split(yt * yt)
    ms = _dot(seg64_ref[...], hi) + _dot(seg64_ref[...], lo)
    yt = yt * lax.rsqrt(ms + EPS) * _tile_lanes(dkn_ref[...], tm)
    parts = []
    for grp in range(wd // HEAD64):
        lo_rows = yt[grp * HEAD64:grp * HEAD64 + half]
        hi_rows = yt[grp * HEAD64 + half:(grp + 1) * HEAD64]
        parts += [hi_rows, lo_rows]
    swapped_t = jnp.concatenate(parts, axis=0)
    reps = wd // HEAD64
    cos_t = jnp.concatenate([cost_ref[...]] * reps, axis=0)
    sin_t = jnp.concatenate([sint_ref[...]] * reps, axis=0)
    put_t(dkt_f, dkt_b, yt * cos_t + swapped_t * sin_t)


def _in_proj(x3d, g, w_row, w_t, cos_tab, sin_tab, cos_t, sin_t, seg64, seg128, dqn, dkn_col, mqn, tm):
    b, t, d = x3d.shape
    wd = BRANCH_WIDTH
    nt = t // tm
    nc = tm // KV_CHUNK
    row_spec = pl.BlockSpec((1, tm, wd), lambda bi, ti: (bi, ti, 0))
    t_spec = pl.BlockSpec((1, wd, tm), lambda bi, ti: (bi, 0, ti))
    tb_spec = pl.BlockSpec((1, nc, wd, KV_CHUNK), lambda bi, ti: (bi, ti, 0, 0))
    row = lambda dt: jax.ShapeDtypeStruct((b, t, wd), dt)
    tr_f = jax.ShapeDtypeStruct((b, wd, t), F32)
    tr_b = jax.ShapeDtypeStruct((b, t // KV_CHUNK, wd, KV_CHUNK), BF16)
    return pl.pallas_call(
        _in_proj_kernel,
        out_shape=(row(BF16), row(BF16), row(F32), row(BF16), row(BF16),
                   tr_f, tr_b, tr_f, tr_b, tr_f, tr_b),
        grid=(b, nt),
        in_specs=[
            pl.BlockSpec((1, tm, d), lambda bi, ti: (bi, ti, 0)),
            _const_spec(g.shape),
            _const_spec(w_row.shape),
            _const_spec(w_t.shape),
            pl.BlockSpec((tm, LANES), lambda bi, ti: (ti, 0)),
            pl.BlockSpec((tm, LANES), lambda bi, ti: (ti, 0)),
            pl.BlockSpec((HEAD64, tm), lambda bi, ti: (0, ti)),
            pl.BlockSpec((HEAD64, tm), lambda bi, ti: (0, ti)),
            _const_spec(seg64.shape),
            _const_spec(seg128.shape),
            _const_spec(dqn.shape),
            _const_spec(dkn_col.shape),
            _const_spec(mqn.shape),
        ],
        out_specs=(row_spec, row_spec, row_spec, row_spec, row_spec,
                   t_spec, tb_spec, t_spec, tb_spec, t_spec, tb_spec),
        compiler_params=_cparams(("parallel", "parallel")),
        name="in_proj",
    )(x3d, g, w_row, w_t, cos_tab, sin_tab, cos_t, sin_t, seg64, seg128, dqn, dkn_col, mqn)


def _mem_kv_kernel(x_ref, g_ref, w_ref, seg128_ref, kn_ref, mk_f, mk_b, mv_f, mv_b):
    h = _rms_rows(x_ref[...], g_ref[...]).astype(BF16)
    wd = BRANCH_WIDTH
    k = _dot(h, w_ref[:, :wd])
    k = k * lax.rsqrt(_split_dot(k * k, seg128_ref[...]) + EPS) * kn_ref[...]
    mk_f[...] = k
    mk_b[...] = k.astype(BF16)
    v = _dot(h, w_ref[:, wd:])
    mv_f[...] = v
    mv_b[...] = v.astype(BF16)


def _mem_kv(mem2d, g, w_kv, seg128, kn):
    m = mem2d.shape[0]
    wd = BRANCH_WIDTH
    full = lambda shape: pl.BlockSpec(shape, lambda i: (0, 0))
    return pl.pallas_call(
        _mem_kv_kernel,
        out_shape=tuple(jax.ShapeDtypeStruct((m, wd), dt) for dt in (F32, BF16, F32, BF16)),
        grid=(1,),
        in_specs=[full(mem2d.shape), full(g.shape), full(w_kv.shape), full(seg128.shape),
                  full(kn.shape)],
        out_specs=tuple(full((m, wd)) for _ in range(4)),
        compiler_params=_cparams(("arbitrary",)),
        name="mem_kv",
    )(mem2d, g, w_kv, seg128, kn)


def _sb_logs(z, vis):
    sp = jnp.log(1.0 + jnp.exp(-jnp.abs(z)))
    log_keep = -(jnp.maximum(z, 0.0) + sp)
    log_take = jnp.minimum(z, 0.0) - sp
    if vis is not None:
        log_keep = jnp.where(vis, log_keep, 0.0)
    return log_keep, log_take


def _sb_weights(log_take, excl, c, vis):
    w = jnp.exp(log_take + excl + _tile_lanes(c, excl.shape[-1]))
    return w if vis is None else jnp.where(vis, w, 0.0)


def _sb_prompt_kernel(q_ref, kt_ref, vt_ref, u_ref, o_ref, acc_ref, c_ref, *, tq):
    qi = pl.program_id(2)
    q = q_ref[0]
    lane = lax.broadcasted_iota(jnp.int32, (tq, LANES), 1)
    first = lane < HEAD64
    zero = jnp.zeros_like(q)
    q_heads = (jnp.where(first, q, zero), jnp.where(first, zero, q))
    acc_ref[...] = jnp.zeros_like(acc_ref)
    c_ref[...] = jnp.zeros_like(c_ref)
    tk = KV_CHUNK

    def tile(kc, masked):
        kt = kt_ref[0, kc]
        vt = vt_ref[0, kc]
        vis = None
        if masked:
            vis = (lax.broadcasted_iota(jnp.int32, (tq, tk), 1)
                   < lax.broadcasted_iota(jnp.int32, (tq, tk), 0))
        pv = []
        for hh in range(2):
            z = _dot(q_heads[hh], kt)
            log_keep, log_take = _sb_logs(z, vis)
            ee = _split_dot(log_keep, u_ref[...])
            c = c_ref[hh]
            w = _sb_weights(log_take, ee[:, :tk], c, vis)
            pv.append(_dot_nt(w.astype(BF16), vt))
            c_ref[hh] = c + ee[:, tk:]
        acc_ref[...] += jnp.where(first, pv[0], pv[1])

    tile(qi, True)

    def cond(state):
        kc, done = state
        return jnp.logical_and(kc >= 0, done == 0)

    def body(state):
        kc, _ = state
        tile(kc, False)
        done = (jnp.max(c_ref[...]) < SB_DONE_LOG).astype(jnp.int32)
        return kc - 1, done

    lax.while_loop(cond, body, (qi - 1, jnp.int32(0)))
    o_ref[0] = acc_ref[...].astype(o_ref.dtype)


def _reverse_cumsum_matrix(tk):
    j = jnp.arange(tk)
    u = (j[:, None] > j[None, :]).astype(BF16)
    return jnp.concatenate([u, jnp.ones((tk, LANES), BF16)], axis=1)


def _sb_prompt(q, kt, vt):
    b, t, wd = q.shape
    n_chunks = kt.shape[1]
    tq = KV_CHUNK
    kernel = functools.partial(_sb_prompt_kernel, tq=tq)
    qspec = pl.BlockSpec((1, tq, LANES), lambda bi, hp, qi: (bi, qi, hp))
    kvspec = pl.BlockSpec((1, n_chunks, LANES, KV_CHUNK), lambda bi, hp, qi: (bi, 0, hp, 0))
    return pl.pallas_call(
        kernel,
        out_shape=jax.ShapeDtypeStruct((b, t, wd), BF16),
        grid=(b, wd // LANES, t // tq),
        in_specs=[qspec, kvspec, kvspec, _const_spec((KV_CHUNK, KV_CHUNK + LANES))],
        out_specs=qspec,
        scratch_shapes=[pltpu.VMEM((tq, LANES), F32), pltpu.VMEM((2, tq, LANES), F32)],
        compiler_params=_cparams(("parallel", "parallel", "parallel")),
        name="sb_prompt",
    )(q, kt, vt, _reverse_cumsum_matrix(KV_CHUNK))


def _lambda_value(lam_ref, lam_init):
    lv = lam_ref[...]
    a = jnp.sum(lv[0:1] * lv[1:2], axis=-1, keepdims=True)
    b = jnp.sum(lv[2:3] * lv[3:4], axis=-1, keepdims=True)
    return jnp.exp(a) - jnp.exp(b) + lam_init


def _diff_prompt_kernel(lam_ref, sub_ref, q_ref, kt_ref, v_ref, o_ref, m_ref, acc_ref,
                        *, tq, lam_init):
    qi = pl.program_id(2)
    q = q_ref[0]
    lane = lax.broadcasted_iota(jnp.int32, (tq, LANES), 1)
    first = lane < HEAD64
    zero = jnp.zeros_like(q)
    qs = jnp.concatenate([jnp.where(first, q, zero), jnp.where(first, zero, q)], axis=0)
    m_ref[...] = jnp.full_like(m_ref, NEG_BIG)
    acc_ref[...] = jnp.zeros_like(acc_ref)
    ck = KV_CHUNK
    cps = DIFF_CHUNKS_PER_STEP

    def step(chunks, v, masked):
        width = ck * len(chunks)
        v_ext = jnp.concatenate([v, jnp.ones_like(v)], axis=1)
        s = [_dot(qs, kt_ref[0, c]) for c in chunks]
        s = s[0] if len(s) == 1 else jnp.concatenate(s, axis=1)
        if masked:
            row = lax.broadcasted_iota(jnp.int32, (2 * tq, width), 0) % tq
            col = lax.broadcasted_iota(jnp.int32, (2 * tq, width), 1) - (width - tq)
            s = jnp.where(col <= row, s, NEG_BIG)
        m_prev = m_ref[...]
        m_new = jnp.maximum(m_prev, jnp.max(s, axis=-1, keepdims=True))
        p = jnp.exp(s - _tile_lanes(m_new, width))
        alpha = jnp.exp(m_prev - m_new)
        acc_ref[...] = _tile_lanes(alpha, 2 * LANES) * acc_ref[...] + _dot(p.astype(BF16), v_ext)
        m_ref[...] = m_new

    def v_rows(chunk, n):
        return v_ref[0, pl.ds(pl.multiple_of(chunk * ck, ck), n * ck), :]

    def body(kt, carry):
        step([cps * kt + i for i in range(cps)], v_rows(cps * kt, cps), False)
        return carry

    qc = tq // ck
    assert cps % qc == 0
    n_prev = qi * qc
    n_full = n_prev // cps
    lax.fori_loop(0, n_full, body, 0)

    base = n_full * cps
    for extra in range(0, cps, qc):
        @pl.when(n_prev - base == extra)
        def _():
            step([base + i for i in range(extra + qc)], v_rows(base, extra + qc), True)

    lam = _lambda_value(lam_ref, lam_init)
    a0, a1 = acc_ref[0:tq], acc_ref[tq:2 * tq]
    o = a0[:, :LANES] / a0[:, LANES:] - lam * (a1[:, :LANES] / a1[:, LANES:])
    o = _rms_rows(o, sub_ref[...]) * (1.0 - lam_init)
    o_ref[0] = o.astype(o_ref.dtype)


def _diff_prompt(lam_vecs, subln, q, kt, v, lam_init):
    b, t, wd = q.shape
    n_chunks = kt.shape[1]
    tq = min(DIFF_Q_CHUNKS * KV_CHUNK, t)
    kernel = functools.partial(_diff_prompt_kernel, tq=tq, lam_init=lam_init)
    qspec = pl.BlockSpec((1, tq, LANES), lambda bi, h, qi: (bi, qi, h))
    ktspec = pl.BlockSpec((1, n_chunks, LANES, KV_CHUNK), lambda bi, h, qi: (bi, 0, h, 0))
    vspec = pl.BlockSpec((1, t, LANES), lambda bi, h, qi: (bi, 0, h))
    return pl.pallas_call(
        kernel,
        out_shape=jax.ShapeDtypeStruct((b, t, wd), BF16),
        grid=(b, wd // LANES, t // tq),
        in_specs=[_const_spec(lam_vecs.shape), _const_spec(subln.shape), qspec, ktspec, vspec],
        out_specs=qspec,
        scratch_shapes=[pltpu.VMEM((2 * tq, LANES), F32), pltpu.VMEM((2 * tq, 2 * LANES), F32)],
        compiler_params=_cparams(("parallel", "parallel", "parallel")),
        name="diff_prompt",
    )(lam_vecs, subln, q, kt, v)


def _softmax_pv(s, v):
    p = jnp.exp(s - jnp.max(s, axis=-1, keepdims=True))
    return _dot(p.astype(BF16), v) / jnp.sum(p, axis=-1, keepdims=True)


def _mem_attend_prompt_kernel(q_ref, k_ref, v_ref, o_ref):
    q, k, v = q_ref[0], k_ref[0], v_ref[0]
    scale = HEAD128 ** -0.5
    outs = []
    for h in range(MEM_HEADS):
        sl = slice(h * HEAD128, (h + 1) * HEAD128)
        outs.append(_softmax_pv(_dot_nt(q[:, sl], k[:, sl]) * scale, v[:, sl]))
    o_ref[0] = jnp.concatenate(outs, axis=-1).astype(o_ref.dtype)


def _mem_attend_prompt(q, k, v, tq):
    g, r, wd = q.shape
    n_mem = k.shape[1]
    qspec = pl.BlockSpec((1, tq, wd), lambda gi, ri: (gi, ri, 0))
    kvspec = pl.BlockSpec((1, n_mem, wd), lambda gi, ri: (gi, 0, 0))
    return pl.pallas_call(
        _mem_attend_prompt_kernel,
        out_shape=jax.ShapeDtypeStruct((g, r, wd), BF16),
        grid=(g, r // tq),
        in_specs=[qspec, kvspec, kvspec],
        out_specs=qspec,
        compiler_params=_cparams(("parallel", "parallel")),
        name="mem_attend_prompt",
    )(q, k, v)


def _mem_attend_sample_kernel(q_ref, k_ref, v_ref, o_ref, *, n_mem):
    q = q_ref[0]
    scale = HEAD128 ** -0.5
    outs = []
    for h in range(MEM_HEADS):
        sl = slice(h * HEAD128, (h + 1) * HEAD128)
        k = k_ref[0, pl.ds(h, n_mem, stride=MEM_HEADS), :].astype(BF16)
        v = v_ref[0, pl.ds(h, n_mem, stride=MEM_HEADS), :].astype(BF16)
        outs.append(_softmax_pv(_dot_nt(q[:, sl], k) * scale, v))
    o_ref[0] = jnp.concatenate(outs, axis=-1).astype(o_ref.dtype)


def _mem_attend_sample(q, k_rows, v_rows, n_mem):
    db, r, wd = q.shape
    qspec = pl.BlockSpec((1, r, wd), lambda bi: (bi, 0, 0))
    kvspec = pl.BlockSpec((1, n_mem * MEM_HEADS, HEAD128), lambda bi: (bi, 0, 0))
    return pl.pallas_call(
        functools.partial(_mem_attend_sample_kernel, n_mem=n_mem),
        out_shape=jax.ShapeDtypeStruct((db, r, wd), BF16),
        grid=(db,),
        in_specs=[qspec, kvspec, kvspec],
        out_specs=qspec,
        compiler_params=_cparams(("parallel",)),
        name="mem_attend_sample",
    )(q, k_rows, v_rows)


def _sb_pages(q, k_refs, v_refs, vis, uext, acc, c):
    rows = q.shape[0]
    n = len(k_refs)
    logs = [_sb_logs(_dot(q, k[0].astype(BF16)), vis) for k in k_refs]
    log_keep = logs[0][0] if n == 1 else jnp.concatenate([lg[0] for lg in logs], axis=0)
    ee = _split_dot(log_keep, uext)
    for i in range(n):
        e = ee[i * rows:(i + 1) * rows]
        w = _sb_weights(logs[i][1], e[:, :PAGE_SIZE], c, vis)
        acc = acc + _dot_nt(w.astype(BF16), v_refs[i][0].astype(BF16))
        c = c + e[:, PAGE_SIZE:]
    return acc, c


def _sb_finalize(acc, gmask, n_new):
    masked = acc * gmask
    return jnp.sum(masked.reshape(n_new, SUBLANES, masked.shape[-1]), axis=1)


def _sb_sample_recent_kernel(pt_ref, q_ref, newk_ref, newv_ref, *rest, n_new, n_recent):
    k_refs, v_refs = rest[:n_recent], rest[n_recent:2 * n_recent]
    u_ref, gmask_ref, o_out, acc_out, c_out, done_out = rest[2 * n_recent:]
    q = q_ref[0]
    rows = q.shape[0]
    t = lax.broadcasted_iota(jnp.int32, (rows, PAGE_SIZE), 0) // SUBLANES
    s = lax.broadcasted_iota(jnp.int32, (rows, PAGE_SIZE), 1)
    acc = jnp.zeros(acc_out.shape[1:], F32)
    c = jnp.zeros(c_out.shape[1:], F32)
    acc, c = _sb_pages(q, [newk_ref], [newv_ref], s < t, u_ref[...], acc, c)
    acc, c = _sb_pages(q, k_refs, v_refs, None, u_ref[...], acc, c)
    o_out[0] = _sb_finalize(acc, gmask_ref[...], n_new)
    acc_out[0] = acc
    c_out[0] = c
    c_max = jnp.max(jnp.max(c, axis=1, keepdims=True), axis=0, keepdims=True)
    done_out[0] = jnp.broadcast_to((c_max < SB_DONE_LOG).astype(jnp.int32), done_out.shape[1:])


def _sb_sample_rest_kernel(pt_ref, done_ref, q_ref, acc_in, c_in, *rest, n_new, g_pages):
    k_refs, v_refs = rest[:g_pages], rest[g_pages:2 * g_pages]
    u_ref, gmask_ref, o_ref, acc_ref, c_ref = rest[2 * g_pages:]
    b = pl.program_id(0)
    j = pl.program_id(1)

    @pl.when(j == 0)
    def _():
        acc_ref[...] = acc_in[0]
        c_ref[...] = c_in[0]

    @pl.when(done_ref[b] == 0)
    def _():
        @pl.when(jnp.max(c_ref[...]) >= SB_DONE_LOG)
        def _():
            acc, c = _sb_pages(q_ref[0], k_refs, v_refs, None, u_ref[...], acc_ref[...], c_ref[...])
            acc_ref[...] = acc
            c_ref[...] = c

    @pl.when(j == pl.num_programs(1) - 1)
    def _():
        o_ref[0] = _sb_finalize(acc_ref[...], gmask_ref[...], n_new)


def _diff_sample_kernel(pt_ref, lam_ref, sub_ref, q_ref, newk_ref, newv_ref, *rest,
                        n_new, g_pages, lam_init):
    k_refs, v_refs = rest[:g_pages], rest[g_pages:2 * g_pages]
    o_ref, m_ref, l_ref, acc_ref = rest[2 * g_pages:]
    j = pl.program_id(1)
    q = q_ref[0]
    rows = q.shape[0]

    def pages(ks, vs, vis):
        n = len(ks)
        s = [_dot(q, k[0].astype(BF16)) for k in ks]
        s = s[0] if n == 1 else jnp.concatenate(s, axis=1)
        if vis is not None:
            s = jnp.where(vis, s, NEG_BIG)
        m_prev = m_ref[...]
        m_new = jnp.maximum(m_prev, jnp.max(s, axis=-1, keepdims=True))
        p = jnp.exp(s - _tile_lanes(m_new, n * PAGE_SIZE))
        alpha = jnp.exp(m_prev - m_new)
        l_ref[...] = alpha * l_ref[...] + jnp.sum(p, axis=-1, keepdims=True)
        pv = []
        for h in range(DIFF_HEADS):
            ph = p[h * SUBLANES:(h + 1) * SUBLANES].astype(BF16)
            tot = None
            for i in range(n):
                vh = vs[i][0, pl.ds(h, PAGE_SIZE, stride=DIFF_HEADS), :].astype(BF16)
                term = _dot(ph[:, i * PAGE_SIZE:(i + 1) * PAGE_SIZE], vh)
                tot = term if tot is None else tot + term
            pv.append(tot)
        acc_ref[...] = alpha * acc_ref[...] + jnp.concatenate(pv, axis=0)
        m_ref[...] = m_new

    @pl.when(j == 0)
    def _():
        m_ref[...] = jnp.full_like(m_ref, NEG_BIG)
        l_ref[...] = jnp.zeros_like(l_ref)
        acc_ref[...] = jnp.zeros_like(acc_ref)
        t = (lax.broadcasted_iota(jnp.int32, (rows, PAGE_SIZE), 0) % SUBLANES) // 2
        s = lax.broadcasted_iota(jnp.int32, (rows, PAGE_SIZE), 1)
        pages([newk_ref], [newv_ref], s <= t)

    pages(k_refs, v_refs, None)

    @pl.when(j == pl.num_programs(1) - 1)
    def _():
        lam = _lambda_value(lam_ref, lam_init)
        r = lax.broadcasted_iota(jnp.int32, (rows, LANES), 0)
        coef = jnp.where(r % 2 == 0, 1.0, -lam)
        acc_ref[...] = acc_ref[...] * (coef / l_ref[...])
        outs = []
        for h in range(DIFF_HEADS):
            o = (acc_ref[pl.ds(h * SUBLANES, n_new, stride=2), :]
                 + acc_ref[pl.ds(h * SUBLANES + 1, n_new, stride=2), :])
            outs.append(_rms_rows(o, sub_ref[...]) * (1.0 - lam_init))
        o_ref[0] = jnp.concatenate(outs, axis=-1).astype(o_ref.dtype)


def _pages_per_step(n_pages):
    return max(g for g in range(1, MAX_PAGES_PER_STEP + 1) if n_pages % g == 0)


def _paged_specs(n_pages, g_pages):
    per_b = lambda shape: pl.BlockSpec(shape, lambda b, j, pt: (b, 0, 0))
    const = lambda shape: pl.BlockSpec(shape, lambda b, j, pt: (0,) * len(shape))

    def page(g):
        return pl.BlockSpec((1, BRANCH_WIDTH, PAGE_SIZE),
                            lambda b, j, pt: (pt[b, n_pages - 1 - (j * g_pages + g)], 0, 0))

    return per_b, const, [page(g) for g in range(g_pages)]


def _sb_sample(page_table, qbd, newk, newv, pool_k, pool_v, gmask, n_new):
    db, rows, wd = qbd.shape
    n_pages = page_table.shape[1]
    assert n_pages >= 2
    n_recent = min(SB_RECENT_PAGES, n_pages // 2)
    n_rest = n_pages - n_recent
    g_pages = max(g for g in range(1, MAX_REST_PAGES_PER_STEP + 1) if n_rest % g == 0)
    uext = _reverse_cumsum_matrix(PAGE_SIZE)
    page_block = (1, wd, PAGE_SIZE)

    per_b = lambda shape: pl.BlockSpec(shape, lambda b, pt: (b, 0, 0))
    recent = [pl.BlockSpec(page_block, lambda b, pt, g=g: (pt[b, n_pages - 1 - g], 0, 0))
              for g in range(n_recent)]
    const1 = lambda shape: pl.BlockSpec(shape, lambda b, pt: (0,) * len(shape))
    o_recent, acc, c, done = pl.pallas_call(
        functools.partial(_sb_sample_recent_kernel, n_new=n_new, n_recent=n_recent),
        out_shape=(jax.ShapeDtypeStruct((db, n_new, wd), F32),
                   jax.ShapeDtypeStruct((db, rows, wd), F32),
                   jax.ShapeDtypeStruct((db, rows, LANES), F32),
                   jax.ShapeDtypeStruct((db, SUBLANES, LANES), jnp.int32)),
        grid_spec=pltpu.PrefetchScalarGridSpec(
            num_scalar_prefetch=1,
            grid=(db,),
            in_specs=[per_b((1, rows, wd)), per_b(page_block), per_b(page_block)]
                     + recent + recent + [const1(uext.shape), const1(gmask.shape)],
            out_specs=(per_b((1, n_new, wd)), per_b((1, rows, wd)), per_b((1, rows, LANES)),
                       per_b((1, SUBLANES, LANES))),
        ),
        compiler_params=_cparams(("parallel",)),
        name="sb_sample_recent",
    )(page_table, qbd, newk, newv, *([pool_k] * n_recent), *([pool_v] * n_recent), uext, gmask)
    done = done[:, 0, 0]

    per_b2 = lambda shape: pl.BlockSpec(shape, lambda b, j, pt, dn: (b, 0, 0))
    const2 = lambda shape: pl.BlockSpec(shape, lambda b, j, pt, dn: (0,) * len(shape))

    def older(g):
        def index(b, j, pt, dn):
            page = pt[b, n_rest - 1 - (j * g_pages + g)]
            return (jnp.where(dn[b] == 0, page, 0), 0, 0)
        return pl.BlockSpec(page_block, index)

    older_specs = [older(g) for g in range(g_pages)]
    rest_call = pl.pallas_call(
        functools.partial(_sb_sample_rest_kernel, n_new=n_new, g_pages=g_pages),
        out_shape=jax.ShapeDtypeStruct((db, n_new, wd), F32),
        grid_spec=pltpu.PrefetchScalarGridSpec(
            num_scalar_prefetch=2,
            grid=(db, n_rest // g_pages),
            in_specs=[per_b2((1, rows, wd)), per_b2((1, rows, wd)), per_b2((1, rows, LANES))]
                     + older_specs + older_specs + [const2(uext.shape), const2(gmask.shape)],
            out_specs=per_b2((1, n_new, wd)),
            scratch_shapes=[pltpu.VMEM((rows, wd), F32), pltpu.VMEM((rows, LANES), F32)],
        ),
        compiler_params=_cparams(("parallel", "arbitrary")),
        name="sb_sample_rest",
    )
    return lax.cond(
        jnp.any(done == 0),
        lambda: rest_call(page_table, done, qbd, acc, c, *([pool_k] * g_pages),
                          *([pool_v] * g_pages), uext, gmask),
        lambda: o_recent)


def _diff_sample(page_table, lam_vecs, subln, qbd, newk, newv, pool_k, pool_v, n_new, lam_init):
    db, rows, wd = qbd.shape
    n_pages = page_table.shape[1]
    g_pages = _pages_per_step(n_pages)
    per_b, const, page_specs = _paged_specs(n_pages, g_pages)
    grid_spec = pltpu.PrefetchScalarGridSpec(
        num_scalar_prefetch=1,
        grid=(db, n_pages // g_pages),
        in_specs=[const(lam_vecs.shape), const(subln.shape),
                  per_b((1, rows, wd)), per_b((1, wd, PAGE_SIZE)), per_b((1, wd, PAGE_SIZE))]
                 + page_specs + page_specs,
        out_specs=per_b((1, n_new, wd)),
        scratch_shapes=[pltpu.VMEM((rows, LANES), F32), pltpu.VMEM((rows, LANES), F32),
                        pltpu.VMEM((rows, LANES), F32)],
    )
    return pl.pallas_call(
        functools.partial(_diff_sample_kernel, n_new=n_new, g_pages=g_pages, lam_init=lam_init),
        out_shape=jax.ShapeDtypeStruct((db, n_new, wd), F32),
        grid_spec=grid_spec,
        compiler_params=_cparams(("parallel", "arbitrary")),
        name="diff_sample",
    )(page_table, lam_vecs, subln, qbd, newk, newv, *([pool_k] * g_pages), *([pool_v] * g_pages))


def _merge_kernel(x_ref, osb_ref, odf_ref, omem_ref, g_ref, wg_ref, bg_ref, wb_ref, wo_ref,
                  o_ref):
    x = x_ref[...]
    h = _rms_rows(x, g_ref[...]).astype(BF16)
    merged = None
    for b, ob_ref in enumerate((osb_ref, odf_ref, omem_ref)):
        gate = _dot(h, wg_ref[:, b * D_MODEL:(b + 1) * D_MODEL]) + bg_ref[b:b + 1, :]
        term = jax.nn.sigmoid(gate) * _dot(ob_ref[...].astype(BF16), wb_ref[b])
        merged = term if merged is None else merged + term
    o_ref[...] = x + _dot(merged.astype(BF16), wo_ref[...])


def _merge(x2d, o_sb, o_diff, o_mem, g, w_gate, b_gate, w_branch, w_out, tm):
    m = x2d.shape[0]
    row = lambda i: (i, 0)
    wide = pl.BlockSpec((tm, D_MODEL), row)
    narrow = pl.BlockSpec((tm, BRANCH_WIDTH), row)
    return pl.pallas_call(
        _merge_kernel,
        out_shape=jax.ShapeDtypeStruct((m, D_MODEL), F32),
        grid=(m // tm,),
        in_specs=[wide, narrow, narrow, narrow, _const_spec(g.shape), _const_spec(w_gate.shape),
                  _const_spec(b_gate.shape), _const_spec(w_branch.shape),
                  _const_spec(w_out.shape)],
        out_specs=wide,
        compiler_params=_cparams(("parallel",)),
        name="merge",
    )(x2d, o_sb, o_diff, o_mem, g, w_gate, b_gate, w_branch, w_out)


def _ffn_chunks(d_ff):
    chunk = MXU_WIDTH if d_ff % MXU_WIDTH == 0 else LANES
    return chunk, d_ff // chunk


def _ffn_prompt_kernel(x_ref, g_ref, wup_ref, cw_ref, cb_ref, wdn_ref, o_ref, tail_ref, abuf_ref,
                       *, tm, d_ff):
    pad = SUBLANES

    @pl.when(pl.program_id(1) == 0)
    def _():
        abuf_ref[0:pad, :] = jnp.zeros((pad, d_ff), F32)

    x = x_ref[0]
    h = _rms_rows(x, g_ref[...]).astype(BF16)
    chunk, n_chunks = _ffn_chunks(d_ff)
    acc = x
    for c in range(n_chunks):
        sl = slice(c * chunk, (c + 1) * chunk)
        abuf_ref[pad:pad + tm, sl] = _dot(h, wup_ref[:, sl])
        conv = (cb_ref[:, sl]
                + cw_ref[0:1, sl] * abuf_ref[pad - 2:pad - 2 + tm, sl]
                + cw_ref[1:2, sl] * abuf_ref[pad - 1:pad - 1 + tm, sl]
                + cw_ref[2:3, sl] * abuf_ref[pad:pad + tm, sl])
        up = _dot(h, wup_ref[:, d_ff + c * chunk:d_ff + (c + 1) * chunk])
        acc = acc + _dot((jax.nn.silu(conv) * up).astype(BF16), wdn_ref[sl, :])
    o_ref[0] = acc
    last = abuf_ref[tm:tm + pad, :]
    tail_ref[0] = last
    abuf_ref[0:pad, :] = last


def _ffn_prompt(x3d, g, w_up, conv_w, conv_b, w_down, tm):
    b, t, d = x3d.shape
    d_ff = w_down.shape[0]
    kernel = functools.partial(_ffn_prompt_kernel, tm=tm, d_ff=d_ff)
    xspec = pl.BlockSpec((1, tm, d), lambda bi, ti: (bi, ti, 0))
    return pl.pallas_call(
        kernel,
        out_shape=(jax.ShapeDtypeStruct((b, t, d), F32),
                   jax.ShapeDtypeStruct((b, SUBLANES, d_ff), F32)),
        grid=(b, t // tm),
        in_specs=[xspec, _const_spec(g.shape), _const_spec(w_up.shape), _const_spec(conv_w.shape),
                  _const_spec(conv_b.shape), _const_spec(w_down.shape)],
        out_specs=(xspec, pl.BlockSpec((1, SUBLANES, d_ff), lambda bi, ti: (bi, 0, 0))),
        scratch_shapes=[pltpu.VMEM((tm + SUBLANES, d_ff), F32)],
        compiler_params=_cparams(("parallel", "arbitrary")),
        name="ffn_prompt",
    )(x3d, g, w_up, conv_w, conv_b, w_down)


def _ffn_sample_kernel(x_ref, g_ref, pre_ref, wa_ref, wu_ref, cw_ref, cb_ref, wdn_ref,
                       o_ref, a_ref, h_ref, *, n_new, db):
    c = pl.program_id(0)

    @pl.when(c == 0)
    def _():
        x = x_ref[...]
        h_ref[...] = _rms_rows(x, g_ref[...]).astype(BF16)
        o_ref[...] = x

    h = h_ref[...]
    a = _dot(h, wa_ref[...])
    a_ref[...] = a
    up = _dot(h, wu_ref[...])
    slabs = [pre_ref[0], pre_ref[1]] + [a[t * db:(t + 1) * db] for t in range(n_new)]
    conv = jnp.concatenate(
        [cb_ref[...] + cw_ref[0:1] * slabs[t] + cw_ref[1:2] * slabs[t + 1] + cw_ref[2:3] * slabs[t + 2]
         for t in range(n_new)], axis=0)
    o_ref[...] += _dot((jax.nn.silu(conv) * up).astype(BF16), wdn_ref[...])


def _ffn_sample(x_tm, g, prefix_tm, w_up, conv_w, conv_b, w_down, n_new, db):
    m, d = x_tm.shape
    d_ff = w_down.shape[0]
    chunk, n_chunks = _ffn_chunks(d_ff)
    kernel = functools.partial(_ffn_sample_kernel, n_new=n_new, db=db)
    fixed = lambda shape: pl.BlockSpec(shape, lambda c: (0,) * len(shape))
    return pl.pallas_call(
        kernel,
        out_shape=(jax.ShapeDtypeStruct((m, d), F32), jax.ShapeDtypeStruct((m, d_ff), F32)),
        grid=(n_chunks,),
        in_specs=[fixed((m, d)), fixed(g.shape),
                  pl.BlockSpec((CONV_WIDTH - 1, db, chunk), lambda c: (0, 0, c)),
                  pl.BlockSpec((d, chunk), lambda c: (0, c)),
                  pl.BlockSpec((d, chunk), lambda c: (0, n_chunks + c)),
                  pl.BlockSpec((CONV_WIDTH, chunk), lambda c: (0, c)),
                  pl.BlockSpec((1, chunk), lambda c: (0, c)),
                  pl.BlockSpec((chunk, d), lambda c: (c, 0))],
        out_specs=(fixed((m, d)), pl.BlockSpec((m, chunk), lambda c: (0, c))),
        scratch_shapes=[pltpu.VMEM((m, d), BF16)],
        compiler_params=_cparams(("arbitrary",)),
        name="ffn_sample",
    )(x_tm, g, prefix_tm, w_up, w_up, conv_w, conv_b, w_down)


def _rope_tables(pos):
    half = HEAD64 // 2
    inv_freq = jnp.power(ROPE_THETA, -jnp.arange(half, dtype=F32) / half)
    ang = pos.astype(F32)[:, None] * inv_freq[None, :]
    cos, sin = jnp.cos(ang), jnp.sin(ang)
    cos = jnp.concatenate([cos, cos], axis=-1)
    sin = jnp.concatenate([-sin, sin], axis=-1)
    reps = LANES // HEAD64
    return jnp.tile(cos, (1, reps)), jnp.tile(sin, (1, reps)), cos.T, sin.T


def _seg_matrix(width, seg):
    i = jnp.arange(width) // seg
    return ((i[:, None] == i[None, :]).astype(F32) / seg).astype(BF16)


def _tile_gain(gain, width):
    return jnp.tile(gain.astype(F32), width // gain.shape[0]).reshape(1, width)


def _block_diag_queries(q, t_idx, grp):
    lane_grp = jnp.arange(q.shape[-1]) // HEAD64
    mask = (lane_grp[None, :] == grp[:, None]).astype(q.dtype)
    return jnp.take(q, t_idx, axis=1) * mask[None]


def _state_from_t(yt, lead, heads):
    n = len(heads)
    y = yt.reshape(tuple(heads) + tuple(lead))
    perm = tuple(range(n, n + len(lead))) + tuple(range(n))
    return jnp.transpose(y, perm)


def kernel(x_prompt, x_sample, cache_sb_k, cache_sb_v, cache_diff_k, cache_diff_v, cache_mem_k, cache_mem_v, state_conv, page_table, mem_prompt, norm_mix, norm_mem, w_in, b_gate, diff_q_norm, diff_k_norm, lambda_q1, lambda_k1, lambda_q2, lambda_k2, diff_subln, w_mem_kv, mem_q_norm, mem_k_norm, w_branch, w_out, norm_ffn, w_ffn_up, conv_w, conv_b, w_ffn_down):
    depth = w_in.shape[0]
    assert depth == 1, "single-layer step"
    l = 0
    lam_init = 0.8 - 0.6 * math.exp(-0.3 * l)
    bsz, seq, d = x_prompt.shape
    db, n_new, _ = x_sample.shape
    assert 2 * n_new == SUBLANES, "sample query rows are packed eight (t, map) rows per head"
    n_pages = page_table.shape[1]
    n_pool = cache_sb_k.shape[1]
    past_len = n_pages * PAGE_SIZE
    n_mem = mem_prompt.shape[1]
    wd = BRANCH_WIDTH
    d_ff = w_ffn_down.shape[1]
    n_attn = N_ATTN_SECTIONS * wd
    ms = db * n_new

    w_in_b = w_in[l].astype(BF16)
    sec = lambda s: w_in_b[:, s * wd:(s + 1) * wd]
    w_row = jnp.concatenate([sec(SEC_SB_Q), sec(SEC_D_Q), sec(SEC_D_V), sec(SEC_M_Q)], axis=1)
    w_t = jnp.concatenate([sec(SEC_SB_K), sec(SEC_SB_V), sec(SEC_D_K)], axis=1).T
    w_gate = w_in_b[:, n_attn:]
    w_branch_b = w_branch[l].astype(BF16)
    w_out_b = w_out[l].astype(BF16)
    w_up_b = w_ffn_up[l].astype(BF16)
    w_down_b = w_ffn_down[l].astype(BF16)
    w_kv_b = w_mem_kv[l].astype(BF16)
    g_mix = norm_mix[l].reshape(1, d)
    g_mem = norm_mem[l].reshape(1, d)
    g_ffn = norm_ffn[l].reshape(1, d)
    seg64, seg128 = _seg_matrix(wd, HEAD64), _seg_matrix(wd, HEAD128)
    dqn = _tile_gain(diff_q_norm[l], wd)
    dkn_col = jnp.broadcast_to(_tile_gain(diff_k_norm[l], wd).reshape(wd, 1), (wd, LANES))
    mqn, mkn = _tile_gain(mem_q_norm[l], wd), _tile_gain(mem_k_norm[l], wd)
    subln = diff_subln[l].reshape(1, HEAD128)
    lam_vecs = jnp.stack([lambda_q1[l], lambda_k1[l], lambda_q2[l], lambda_k2[l]]).astype(F32)
    cw, cb = conv_w[l], conv_b[l].reshape(1, d_ff)

    def in_proj(x3d, pos, tm):
        cos_r, sin_r, cos_c, sin_c = _rope_tables(pos)
        return _in_proj(x3d, g_mix, w_row, w_t, cos_r, sin_r, cos_c, sin_c, seg64, seg128,
                        dqn, dkn_col, mqn, tm)

    tm = min(512, seq)
    (sbq_b, dq_b, dv_f, dv_b, mq_b, sbkt_f, sbkt_b, sbvt_f, sbvt_b, dkt_f, dkt_b) = in_proj(
        x_prompt, jnp.arange(seq), tm)
    mk_f, mk_b, mv_f, mv_b = _mem_kv(mem_prompt.reshape(bsz * n_mem, d), g_mem, w_kv_b, seg128, mkn)

    o_sb = _sb_prompt(sbq_b, sbkt_b, sbvt_b)
    o_diff = _diff_prompt(lam_vecs, subln, dq_b, dkt_b, dv_b, lam_init)
    o_mem = _mem_attend_prompt(mq_b, mk_b.reshape(bsz, n_mem, wd), mv_b.reshape(bsz, n_mem, wd), tm)
    x1 = _merge(x_prompt.reshape(bsz * seq, d), o_sb.reshape(-1, wd), o_diff.reshape(-1, wd),
                o_mem.reshape(-1, wd), g_mix, w_gate, b_gate[l], w_branch_b, w_out_b, tm)
    y_prompt, conv_tail = _ffn_prompt(x1.reshape(bsz, seq, d), g_ffn, w_up_b, cw, cb, w_down_b, tm)

    pos_s = jnp.tile(past_len + jnp.arange(n_new), db)
    (ssbq_b, sdq_b, sdv_f, _, smq_b, ssbkt_f, _, ssbvt_f, _, sdkt_f, _) = in_proj(
        x_sample.reshape(1, ms, d), pos_s, ms)

    s3 = lambda a: a.reshape(db, n_new, wd)

    def new_t_page(yt):
        y = yt.reshape(wd, db, n_new).transpose(1, 0, 2)
        return jnp.pad(y, ((0, 0), (0, 0), (0, PAGE_SIZE - n_new)))

    r = jnp.arange(n_new * SUBLANES)
    t_sb, grp_sb = r // SUBLANES, r % SUBLANES
    t_df, grp_df = (r % SUBLANES) // 2, 2 * (r // SUBLANES) + r % 2
    gmask_sb = ((jnp.arange(wd) // HEAD64)[None, :] == grp_sb[:, None]).astype(F32)

    t_pages = lambda c: jnp.swapaxes(c[l].reshape(n_pool, PAGE_SIZE, wd), 1, 2)
    row_pages = lambda c: c[l].reshape(c.shape[1], -1, HEAD128)

    so_sb = _sb_sample(page_table, _block_diag_queries(s3(ssbq_b), t_sb, grp_sb),
                       new_t_page(ssbkt_f), new_t_page(ssbvt_f),
                       t_pages(cache_sb_k), t_pages(cache_sb_v), gmask_sb, n_new)
    new_dv = jnp.pad(sdv_f.reshape(db, n_new * DIFF_HEADS, HEAD128),
                     ((0, 0), (0, (PAGE_SIZE - n_new) * DIFF_HEADS), (0, 0)))
    so_diff = _diff_sample(page_table, lam_vecs, subln, _block_diag_queries(s3(sdq_b), t_df, grp_df),
                           new_t_page(sdkt_f), new_dv,
                           t_pages(cache_diff_k), row_pages(cache_diff_v), n_new, lam_init)
    q_rows = 2 * SUBLANES
    q_pad = jnp.pad(s3(smq_b), ((0, 0), (0, q_rows - n_new), (0, 0)))
    so_mem = _mem_attend_sample(q_pad, row_pages(cache_mem_k), row_pages(cache_mem_v), n_mem)[:, :n_new]
    xs1 = _merge(x_sample.reshape(ms, d), so_sb.reshape(ms, wd), so_diff.reshape(ms, wd),
                 so_mem.reshape(ms, wd), g_mix, w_gate, b_gate[l], w_branch_b, w_out_b, ms)
    xs1_tm = xs1.reshape(db, n_new, d).transpose(1, 0, 2).reshape(ms, d)
    prefix_tm = state_conv[l].transpose(1, 0, 2)
    ys_tm, a_tm = _ffn_sample(xs1_tm, g_ffn, prefix_tm, w_up_b, cw, cb, w_down_b, n_new, db)
    y_sample = ys_tm.reshape(n_new, db, d).transpose(1, 0, 2)
    a_ext = jnp.concatenate([prefix_tm, a_tm.reshape(n_new, db, d_ff)], axis=0)
    conv_s = a_ext[n_new:].transpose(1, 0, 2)

    st = lambda a, *shape: a.reshape((1,) + shape)
    sb_heads, df_heads = (SB_HEADS, HEAD64), (DIFF_HEADS, 2, HEAD64)
    return (
        y_prompt, y_sample,
        jnp.transpose(sbkt_f.reshape(bsz, SB_HEADS, HEAD64, seq), (0, 3, 1, 2))[None],
        jnp.transpose(sbvt_f.reshape(bsz, SB_HEADS, HEAD64, seq), (0, 3, 1, 2))[None],
        jnp.transpose(dkt_f.reshape(bsz, DIFF_HEADS, 2, HEAD64, seq), (0, 4, 1, 2, 3))[None],
        st(dv_f, bsz, seq, DIFF_HEADS, HEAD128),
        st(mk_f, bsz, n_mem, MEM_HEADS, HEAD128), st(mv_f, bsz, n_mem, MEM_HEADS, HEAD128),
        st(conv_tail[:, SUBLANES - (CONV_WIDTH - 1):], bsz, CONV_WIDTH - 1, d_ff),
        _state_from_t(ssbkt_f, (db, n_new), sb_heads)[None],
        _state_from_t(ssbvt_f, (db, n_new), sb_heads)[None],
        _state_from_t(sdkt_f, (db, n_new), df_heads)[None],
        st(sdv_f, db, n_new, DIFF_HEADS, HEAD128),
        st(conv_s, db, CONV_WIDTH - 1, d_ff),
    )
```

```python
import functools
import math

import jax
import jax.numpy as jnp
from jax import lax
from jax.experimental import pallas as pl
from jax.experimental.pallas import tpu as pltpu

F32 = jnp.float32
BF16 = jnp.bfloat16

D_MODEL = 1024
SB_HEADS = 8
DIFF_HEADS = 4
MEM_HEADS = 4
HEAD64 = 64
HEAD128 = 128
BRANCH_WIDTH = 512
N_BRANCH = 3
PAGE_SIZE = 128
CONV_WIDTH = 3
ROPE_THETA = 10000.0
EPS = 1e-6
SEC_SB_Q, SEC_SB_K, SEC_SB_V, SEC_D_Q, SEC_D_K, SEC_D_V, SEC_M_Q = range(7)
N_ATTN_SECTIONS = 7

LANES = 128
SUBLANES = 8
MXU_WIDTH = 256
VMEM_LIMIT = 56 * 1024 * 1024
KV_CHUNK = MXU_WIDTH
MAX_PAGES_PER_STEP = 16
SB_RECENT_PAGES = 4
MAX_REST_PAGES_PER_STEP = 16
MEM_SAMPLE_BATCH_PER_STEP = 4
DIFF_CHUNKS_PER_STEP = 4
DIFF_Q_CHUNKS = 4

NEG_BIG = -1e30
SB_DONE_LOG = -110.0

_NT = (((1,), (1,)), ((), ()))


def _dot(a, b):
    return jnp.dot(a, b, preferred_element_type=F32)


def _dot_nt(a, b):
    return lax.dot_general(a, b, _NT, preferred_element_type=F32)


def _split(x):
    hi = x.astype(BF16)
    return hi, (x - hi.astype(F32)).astype(BF16)


def _split_dot(x, m):
    hi, lo = _split(x)
    n = x.shape[0]
    both = _dot(jnp.concatenate([hi, lo], axis=0), m)
    return both[:n] + both[n:]


def _rms_rows(x, g):
    ms = jnp.mean(x * x, axis=-1, keepdims=True)
    return x * lax.rsqrt(ms + EPS) * g


def _tile_lanes(x, width):
    reps = width // x.shape[-1]
    return x if reps == 1 else jnp.concatenate([x] * reps, axis=-1)


def _cparams(sem, limit=VMEM_LIMIT):
    return pltpu.CompilerParams(dimension_semantics=sem, vmem_limit_bytes=limit)


def _const_spec(shape):
    nd = len(shape)
    return pl.BlockSpec(shape, lambda *_: (0,) * nd, pipeline_mode=pl.Buffered(1))


def _in_proj_kernel(x_ref, g_ref, wrow_ref, wt_ref, cos_ref, sin_ref, cost_ref, sint_ref,
                    seg64_ref, seg128_ref, dqn_ref, dkn_ref, mqn_ref,
                    sbq_b, dq_b, dv_f, dv_b, mq_b, sbkt_f, sbkt_b, sbvt_f, sbvt_b, dkt_f, dkt_b):
    h = _rms_rows(x_ref[0], g_ref[...]).astype(BF16)
    tm = h.shape[0]
    wd = BRANCH_WIDTH

    def proj(s):
        return _dot(h, wrow_ref[:, s * wd:(s + 1) * wd])

    def proj_t(s):
        return _dot_nt(wt_ref[s * wd:(s + 1) * wd, :], h)

    def put_t(f_ref, b_ref, yt):
        f_ref[0] = yt
        for c in range(tm // KV_CHUNK):
            b_ref[0, c] = yt[:, c * KV_CHUNK:(c + 1) * KV_CHUNK].astype(BF16)

    scale64 = HEAD64 ** -0.5
    half = HEAD64 // 2

    sbq_b[0] = (proj(0) * scale64).astype(BF16)

    y = proj(1)
    y = y * lax.rsqrt(_split_dot(y * y, seg64_ref[...]) + EPS) * dqn_ref[...]
    lane = lax.broadcasted_iota(jnp.int32, y.shape, 1)
    swapped = jnp.where((lane % HEAD64) < half, pltpu.roll(y, wd - half, 1), pltpu.roll(y, half, 1))
    y = y * _tile_lanes(cos_ref[...], wd) + swapped * _tile_lanes(sin_ref[...], wd)
    dq_b[0] = (y * scale64).astype(BF16)

    y = proj(2)
    dv_f[0] = y
    dv_b[0] = y.astype(BF16)

    y = proj(3)
    y = y * lax.rsqrt(_split_dot(y * y, seg128_ref[...]) + EPS) * mqn_ref[...]
    mq_b[0] = y.astype(BF16)

    put_t(sbkt_f, sbkt_b, proj_t(0))
    put_t(sbvt_f, sbvt_b, proj_t(1))

    yt = proj_t(2)
    hi, lo = _split(yt * yt)
    ms = _dot(seg64_ref[...], hi) + _dot(seg64_ref[...], lo)
    yt = yt * lax.rsqrt(ms + EPS) * _tile_lanes(dkn_ref[...], tm)
    parts = []
    for grp in range(wd // HEAD64):
        lo_rows = yt[grp * HEAD64:grp * HEAD64 + half]
        hi_rows = yt[grp * HEAD64 + half:(grp + 1) * HEAD64]
        parts += [hi_rows, lo_rows]
    swapped_t = jnp.concatenate(parts, axis=0)
    reps = wd // HEAD64
    cos_t = jnp.concatenate([cost_ref[...]] * reps, axis=0)
    sin_t = jnp.concatenate([sint_ref[...]] * reps, axis=0)
    put_t(dkt_f, dkt_b, yt * cos_t + swapped_t * sin_t)


def _in_proj(x3d, g, w_row, w_t, cos_tab, sin_tab, cos_t, sin_t, seg64, seg128, dqn, dkn_col, mqn, tm):
    b, t, d = x3d.shape
    wd = BRANCH_WIDTH
    nt = t // tm
    nc = tm // KV_CHUNK
    row_spec = pl.BlockSpec((1, tm, wd), lambda bi, ti: (bi, ti, 0))
    t_spec = pl.BlockSpec((1, wd, tm), lambda bi, ti: (bi, 0, ti))
    tb_spec = pl.BlockSpec((1, nc, wd, KV_CHUNK), lambda bi, ti: (bi, ti, 0, 0))
    row = lambda dt: jax.ShapeDtypeStruct((b, t, wd), dt)
    tr_f = jax.ShapeDtypeStruct((b, wd, t), F32)
    tr_b = jax.ShapeDtypeStruct((b, t // KV_CHUNK, wd, KV_CHUNK), BF16)
    return pl.pallas_call(
        _in_proj_kernel,
        out_shape=(row(BF16), row(BF16), row(F32), row(BF16), row(BF16),
                   tr_f, tr_b, tr_f, tr_b, tr_f, tr_b),
        grid=(b, nt),
        in_specs=[
            pl.BlockSpec((1, tm, d), lambda bi, ti: (bi, ti, 0)),
            _const_spec(g.shape),
            _const_spec(w_row.shape),
            _const_spec(w_t.shape),
            pl.BlockSpec((tm, LANES), lambda bi, ti: (ti, 0)),
            pl.BlockSpec((tm, LANES), lambda bi, ti: (ti, 0)),
            pl.BlockSpec((HEAD64, tm), lambda bi, ti: (0, ti)),
            pl.BlockSpec((HEAD64, tm), lambda bi, ti: (0, ti)),
            _const_spec(seg64.shape),
            _const_spec(seg128.shape),
            _const_spec(dqn.shape),
            _const_spec(dkn_col.shape),
            _const_spec(mqn.shape),
        ],
        out_specs=(row_spec, row_spec, row_spec, row_spec, row_spec,
                   t_spec, tb_spec, t_spec, tb_spec, t_spec, tb_spec),
        compiler_params=_cparams(("parallel", "parallel")),
        name="in_proj",
    )(x3d, g, w_row, w_t, cos_tab, sin_tab, cos_t, sin_t, seg64, seg128, dqn, dkn_col, mqn)


def _mem_kv_kernel(x_ref, g_ref, w_ref, seg128_ref, kn_ref, mk_f, mk_b, mv_f, mv_b):
    h = _rms_rows(x_ref[...], g_ref[...]).astype(BF16)
    wd = BRANCH_WIDTH
    k = _dot(h, w_ref[:, :wd])
    k = k * lax.rsqrt(_split_dot(k * k, seg128_ref[...]) + EPS) * kn_ref[...]
    mk_f[...] = k
    mk_b[...] = k.astype(BF16)
    v = _dot(h, w_ref[:, wd:])
    mv_f[...] = v
    mv_b[...] = v.astype(BF16)


def _mem_kv(mem2d, g, w_kv, seg128, kn):
    m = mem2d.shape[0]
    wd = BRANCH_WIDTH
    full = lambda shape: pl.BlockSpec(shape, lambda i: (0, 0))
    return pl.pallas_call(
        _mem_kv_kernel,
        out_shape=tuple(jax.ShapeDtypeStruct((m, wd), dt) for dt in (F32, BF16, F32, BF16)),
        grid=(1,),
        in_specs=[full(mem2d.shape), full(g.shape), full(w_kv.shape), full(seg128.shape),
                  full(kn.shape)],
        out_specs=tuple(full((m, wd)) for _ in range(4)),
        compiler_params=_cparams(("arbitrary",)),
        name="mem_kv",
    )(mem2d, g, w_kv, seg128, kn)


def _sb_logs(z, vis):
    sp = jnp.log(1.0 + jnp.exp(-jnp.abs(z)))
    log_keep = -(jnp.maximum(z, 0.0) + sp)
    log_take = jnp.minimum(z, 0.0) - sp
    if vis is not None:
        log_keep = jnp.where(vis, log_keep, 0.0)
    return log_keep, log_take


def _sb_weights(log_take, excl, c, vis):
    w = jnp.exp(log_take + excl + _tile_lanes(c, excl.shape[-1]))
    return w if vis is None else jnp.where(vis, w, 0.0)


def _sb_prompt_kernel(q_ref, kt_ref, vt_ref, u_ref, o_ref, acc_ref, c_ref, *, tq):
    qi = pl.program_id(2)
    q = q_ref[0]
    lane = lax.broadcasted_iota(jnp.int32, (tq, LANES), 1)
    first = lane < HEAD64
    zero = jnp.zeros_like(q)
    qs = jnp.concatenate([jnp.where(first, q, zero), jnp.where(first, zero, q)], axis=0)
    acc_ref[...] = jnp.zeros_like(acc_ref)
    c_ref[...] = jnp.zeros_like(c_ref)
    tk = KV_CHUNK

    def tile(kc, masked):
        kt = kt_ref[0, kc]
        vt = vt_ref[0, kc]
        vis = None
        if masked:
            vis = (lax.broadcasted_iota(jnp.int32, (2 * tq, tk), 1)
                   < lax.broadcasted_iota(jnp.int32, (2 * tq, tk), 0) % tq)
        log_keep, log_take = _sb_logs(_dot(qs, kt), vis)
        ee = _split_dot(log_keep, u_ref[...])
        c = c_ref[...]
        w = _sb_weights(log_take, ee[:, :tk], c, vis)
        acc_ref[...] += _dot_nt(w.astype(BF16), vt)
        c_ref[...] = c + ee[:, tk:]

    tile(qi, True)

    def cond(state):
        kc, done = state
        return jnp.logical_and(kc >= 0, done == 0)

    def body(state):
        kc, _ = state
        tile(kc, False)
        done = (jnp.max(c_ref[...]) < SB_DONE_LOG).astype(jnp.int32)
        return kc - 1, done

    lax.while_loop(cond, body, (qi - 1, jnp.int32(0)))
    o_ref[0] = jnp.where(first, acc_ref[0:tq], acc_ref[tq:2 * tq]).astype(o_ref.dtype)


def _reverse_cumsum_matrix(tk):
    j = jnp.arange(tk)
    u = (j[:, None] > j[None, :]).astype(BF16)
    return jnp.concatenate([u, jnp.ones((tk, LANES), BF16)], axis=1)


def _sb_prompt(q, kt, vt):
    b, t, wd = q.shape
    n_chunks = kt.shape[1]
    tq = KV_CHUNK
    kernel = functools.partial(_sb_prompt_kernel, tq=tq)
    qspec = pl.BlockSpec((1, tq, LANES), lambda bi, hp, qi: (bi, qi, hp))
    kvspec = pl.BlockSpec((1, n_chunks, LANES, KV_CHUNK), lambda bi, hp, qi: (bi, 0, hp, 0))
    return pl.pallas_call(
        kernel,
        out_shape=jax.ShapeDtypeStruct((b, t, wd), BF16),
        grid=(b, wd // LANES, t // tq),
        in_specs=[qspec, kvspec, kvspec, _const_spec((KV_CHUNK, KV_CHUNK + LANES))],
        out_specs=qspec,
        scratch_shapes=[pltpu.VMEM((2 * tq, LANES), F32), pltpu.VMEM((2 * tq, LANES), F32)],
        compiler_params=_cparams(("parallel", "parallel", "parallel")),
        name="sb_prompt",
    )(q, kt, vt, _reverse_cumsum_matrix(KV_CHUNK))


def _lambda_value(lam_ref, lam_init):
    lv = lam_ref[...]
    a = jnp.sum(lv[0:1] * lv[1:2], axis=-1, keepdims=True)
    b = jnp.sum(lv[2:3] * lv[3:4], axis=-1, keepdims=True)
    return jnp.exp(a) - jnp.exp(b) + lam_init


def _diff_prompt_kernel(lam_ref, sub_ref, q_ref, kt_ref, v_ref, o_ref, m_ref, acc_ref,
                        *, tq, lam_init):
    qi = pl.program_id(2)
    q = q_ref[0]
    lane = lax.broadcasted_iota(jnp.int32, (tq, LANES), 1)
    first = lane < HEAD64
    zero = jnp.zeros_like(q)
    qs = jnp.concatenate([jnp.where(first, q, zero), jnp.where(first, zero, q)], axis=0)
    m_ref[...] = jnp.full_like(m_ref, NEG_BIG)
    acc_ref[...] = jnp.zeros_like(acc_ref)
    ck = KV_CHUNK
    cps = DIFF_CHUNKS_PER_STEP

    def step(chunks, v, masked):
        width = ck * len(chunks)
        v_ext = jnp.concatenate([v, jnp.ones_like(v)], axis=1)
        s = [_dot(qs, kt_ref[0, c]) for c in chunks]
        s = s[0] if len(s) == 1 else jnp.concatenate(s, axis=1)
        if masked:
            row = lax.broadcasted_iota(jnp.int32, (2 * tq, width), 0) % tq
            col = lax.broadcasted_iota(jnp.int32, (2 * tq, width), 1) - (width - tq)
            s = jnp.where(col <= row, s, NEG_BIG)
        m_prev = m_ref[...]
        m_new = jnp.maximum(m_prev, jnp.max(s, axis=-1, keepdims=True))
        p = jnp.exp(s - _tile_lanes(m_new, width))
        alpha = jnp.exp(m_prev - m_new)
        acc_ref[...] = _tile_lanes(alpha, 2 * LANES) * acc_ref[...] + _dot(p.astype(BF16), v_ext)
        m_ref[...] = m_new

    def v_rows(chunk, n):
        return v_ref[0, pl.ds(pl.multiple_of(chunk * ck, ck), n * ck), :]

    def body(kt, carry):
        step([cps * kt + i for i in range(cps)], v_rows(cps * kt, cps), False)
        return carry

    qc = tq // ck
    assert cps % qc == 0
    n_prev = qi * qc
    n_full = n_prev // cps
    lax.fori_loop(0, n_full, body, 0)

    base = n_full * cps
    for extra in range(0, cps, qc):
        @pl.when(n_prev - base == extra)
        def _():
            step([base + i for i in range(extra + qc)], v_rows(base, extra + qc), True)

    lam = _lambda_value(lam_ref, lam_init)
    a0, a1 = acc_ref[0:tq], acc_ref[tq:2 * tq]
    o = a0[:, :LANES] / a0[:, LANES:] - lam * (a1[:, :LANES] / a1[:, LANES:])
    o = _rms_rows(o, sub_ref[...]) * (1.0 - lam_init)
    o_ref[0] = o.astype(o_ref.dtype)


def _diff_prompt(lam_vecs, subln, q, kt, v, lam_init):
    b, t, wd = q.shape
    n_chunks = kt.shape[1]
    tq = min(DIFF_Q_CHUNKS * KV_CHUNK, t)
    kernel = functools.partial(_diff_prompt_kernel, tq=tq, lam_init=lam_init)
    qspec = pl.BlockSpec((1, tq, LANES), lambda bi, h, qi: (bi, qi, h))
    ktspec = pl.BlockSpec((1, n_chunks, LANES, KV_CHUNK), lambda bi, h, qi: (bi, 0, h, 0))
    vspec = pl.BlockSpec((1, t, LANES), lambda bi, h, qi: (bi, 0, h))
    return pl.pallas_call(
        kernel,
        out_shape=jax.ShapeDtypeStruct((b, t, wd), BF16),
        grid=(b, wd // LANES, t // tq),
        in_specs=[_const_spec(lam_vecs.shape), _const_spec(subln.shape), qspec, ktspec, vspec],
        out_specs=qspec,
        scratch_shapes=[pltpu.VMEM((2 * tq, LANES), F32), pltpu.VMEM((2 * tq, 2 * LANES), F32)],
        compiler_params=_cparams(("parallel", "parallel", "parallel")),
        name="diff_prompt",
    )(lam_vecs, subln, q, kt, v)


def _softmax_pv(s, v):
    p = jnp.exp(s - jnp.max(s, axis=-1, keepdims=True))
    return _dot(p.astype(BF16), v) / jnp.sum(p, axis=-1, keepdims=True)


def _mem_attend_prompt_kernel(q_ref, k_ref, v_ref, o_ref):
    q, k, v = q_ref[0], k_ref[0], v_ref[0]
    scale = HEAD128 ** -0.5
    outs = []
    for h in range(MEM_HEADS):
        sl = slice(h * HEAD128, (h + 1) * HEAD128)
        outs.append(_softmax_pv(_dot_nt(q[:, sl], k[:, sl]) * scale, v[:, sl]))
    o_ref[0] = jnp.concatenate(outs, axis=-1).astype(o_ref.dtype)


def _mem_attend_prompt(q, k, v, tq):
    g, r, wd = q.shape
    n_mem = k.shape[1]
    qspec = pl.BlockSpec((1, tq, wd), lambda gi, ri: (gi, ri, 0))
    kvspec = pl.BlockSpec((1, n_mem, wd), lambda gi, ri: (gi, 0, 0))
    return pl.pallas_call(
        _mem_attend_prompt_kernel,
        out_shape=jax.ShapeDtypeStruct((g, r, wd), BF16),
        grid=(g, r // tq),
        in_specs=[qspec, kvspec, kvspec],
        out_specs=qspec,
        compiler_params=_cparams(("parallel", "parallel")),
        name="mem_attend_prompt",
    )(q, k, v)


def _mem_attend_sample_kernel(q_ref, k_ref, v_ref, o_ref, *, n_mem):
    scale = HEAD128 ** -0.5
    for e in range(q_ref.shape[0]):
        q = q_ref[e]
        outs = []
        for h in range(MEM_HEADS):
            sl = slice(h * HEAD128, (h + 1) * HEAD128)
            k = k_ref[e, pl.ds(h, n_mem, stride=MEM_HEADS), :].astype(BF16)
            v = v_ref[e, pl.ds(h, n_mem, stride=MEM_HEADS), :].astype(BF16)
            outs.append(_softmax_pv(_dot_nt(q[:, sl], k) * scale, v))
        o_ref[e] = jnp.concatenate(outs, axis=-1).astype(o_ref.dtype)


def _mem_attend_sample(q, k_rows, v_rows, n_mem):
    db, r, wd = q.shape
    per_step = max(g for g in range(1, MEM_SAMPLE_BATCH_PER_STEP + 1) if db % g == 0)
    qspec = pl.BlockSpec((per_step, r, wd), lambda bi: (bi, 0, 0))
    kvspec = pl.BlockSpec((per_step, n_mem * MEM_HEADS, HEAD128), lambda bi: (bi, 0, 0))
    return pl.pallas_call(
        functools.partial(_mem_attend_sample_kernel, n_mem=n_mem),
        out_shape=jax.ShapeDtypeStruct((db, r, wd), BF16),
        grid=(db // per_step,),
        in_specs=[qspec, kvspec, kvspec],
        out_specs=qspec,
        compiler_params=_cparams(("parallel",)),
        name="mem_attend_sample",
    )(q, k_rows, v_rows)


def _sb_pages(q, k_refs, v_refs, vis, uext, acc, c):
    rows = q.shape[0]
    n = len(k_refs)
    logs = [_sb_logs(_dot(q, k[0].astype(BF16)), vis) for k in k_refs]
    log_keep = logs[0][0] if n == 1 else jnp.concatenate([lg[0] for lg in logs], axis=0)
    ee = _split_dot(log_keep, uext)
    for i in range(n):
        e = ee[i * rows:(i + 1) * rows]
        w = _sb_weights(logs[i][1], e[:, :PAGE_SIZE], c, vis)
        acc = acc + _dot_nt(w.astype(BF16), v_refs[i][0].astype(BF16))
        c = c + e[:, PAGE_SIZE:]
    return acc, c


def _sb_finalize(acc, gmask, n_new):
    masked = acc * gmask
    return jnp.sum(masked.reshape(n_new, SUBLANES, masked.shape[-1]), axis=1)


def _sb_sample_recent_kernel(pt_ref, q_ref, newk_ref, newv_ref, *rest, n_new, n_recent):
    k_refs, v_refs = rest[:n_recent], rest[n_recent:2 * n_recent]
    u_ref, gmask_ref, o_out, acc_out, c_out, done_out = rest[2 * n_recent:]
    q = q_ref[0]
    rows = q.shape[0]
    t = lax.broadcasted_iota(jnp.int32, (rows, PAGE_SIZE), 0) // SUBLANES
    s = lax.broadcasted_iota(jnp.int32, (rows, PAGE_SIZE), 1)
    acc = jnp.zeros(acc_out.shape[1:], F32)
    c = jnp.zeros(c_out.shape[1:], F32)
    acc, c = _sb_pages(q, [newk_ref], [newv_ref], s < t, u_ref[...], acc, c)
    acc, c = _sb_pages(q, k_refs, v_refs, None, u_ref[...], acc, c)
    o_out[0] = _sb_finalize(acc, gmask_ref[...], n_new)
    acc_out[0] = acc
    c_out[0] = c
    c_max = jnp.max(jnp.max(c, axis=1, keepdims=True), axis=0, keepdims=True)
    done_out[0] = jnp.broadcast_to((c_max < SB_DONE_LOG).astype(jnp.int32), done_out.shape[1:])


def _sb_sample_rest_kernel(pt_ref, done_ref, q_ref, acc_in, c_in, *rest, n_new, g_pages):
    k_refs, v_refs = rest[:g_pages], rest[g_pages:2 * g_pages]
    u_ref, gmask_ref, o_ref, acc_ref, c_ref = rest[2 * g_pages:]
    b = pl.program_id(0)
    j = pl.program_id(1)

    @pl.when(j == 0)
    def _():
        acc_ref[...] = acc_in[0]
        c_ref[...] = c_in[0]

    @pl.when(done_ref[b] == 0)
    def _():
        @pl.when(jnp.max(c_ref[...]) >= SB_DONE_LOG)
        def _():
            acc, c = _sb_pages(q_ref[0], k_refs, v_refs, None, u_ref[...], acc_ref[...], c_ref[...])
            acc_ref[...] = acc
            c_ref[...] = c

    @pl.when(j == pl.num_programs(1) - 1)
    def _():
        o_ref[0] = _sb_finalize(acc_ref[...], gmask_ref[...], n_new)


def _diff_sample_kernel(pt_ref, lam_ref, sub_ref, q_ref, newk_ref, newv_ref, *rest,
                        n_new, g_pages, lam_init):
    k_refs, v_refs = rest[:g_pages], rest[g_pages:2 * g_pages]
    o_ref, m_ref, l_ref, acc_ref = rest[2 * g_pages:]
    j = pl.program_id(1)
    q = q_ref[0]
    rows = q.shape[0]

    def pages(ks, vs, vis):
        n = len(ks)
        s = [_dot(q, k[0].astype(BF16)) for k in ks]
        s = s[0] if n == 1 else jnp.concatenate(s, axis=1)
        if vis is not None:
            s = jnp.where(vis, s, NEG_BIG)
        m_prev = m_ref[...]
        m_new = jnp.maximum(m_prev, jnp.max(s, axis=-1, keepdims=True))
        p = jnp.exp(s - _tile_lanes(m_new, n * PAGE_SIZE))
        alpha = jnp.exp(m_prev - m_new)
        l_ref[...] = alpha * l_ref[...] + jnp.sum(p, axis=-1, keepdims=True)
        pv = []
        for h in range(DIFF_HEADS):
            ph = p[h * SUBLANES:(h + 1) * SUBLANES].astype(BF16)
            tot = None
            for i in range(n):
                vh = vs[i][0, pl.ds(h, PAGE_SIZE, stride=DIFF_HEADS), :].astype(BF16)
                term = _dot(ph[:, i * PAGE_SIZE:(i + 1) * PAGE_SIZE], vh)
                tot = term if tot is None else tot + term
            pv.append(tot)
        acc_ref[...] = alpha * acc_ref[...] + jnp.concatenate(pv, axis=0)
        m_ref[...] = m_new

    @pl.when(j == 0)
    def _():
        m_ref[...] = jnp.full_like(m_ref, NEG_BIG)
        l_ref[...] = jnp.zeros_like(l_ref)
        acc_ref[...] = jnp.zeros_like(acc_ref)
        t = (lax.broadcasted_iota(jnp.int32, (rows, PAGE_SIZE), 0) % SUBLANES) // 2
        s = lax.broadcasted_iota(jnp.int32, (rows, PAGE_SIZE), 1)
        pages([newk_ref], [newv_ref], s <= t)

    pages(k_refs, v_refs, None)

    @pl.when(j == pl.num_programs(1) - 1)
    def _():
        lam = _lambda_value(lam_ref, lam_init)
        r = lax.broadcasted_iota(jnp.int32, (rows, LANES), 0)
        coef = jnp.where(r % 2 == 0, 1.0, -lam)
        acc_ref[...] = acc_ref[...] * (coef / l_ref[...])
        outs = []
        for h in range(DIFF_HEADS):
            o = (acc_ref[pl.ds(h * SUBLANES, n_new, stride=2), :]
                 + acc_ref[pl.ds(h * SUBLANES + 1, n_new, stride=2), :])
            outs.append(_rms_rows(o, sub_ref[...]) * (1.0 - lam_init))
        o_ref[0] = jnp.concatenate(outs, axis=-1).astype(o_ref.dtype)


def _pages_per_step(n_pages):
    return max(g for g in range(1, MAX_PAGES_PER_STEP + 1) if n_pages % g == 0)


def _paged_specs(n_pages, g_pages):
    per_b = lambda shape: pl.BlockSpec(shape, lambda b, j, pt: (b, 0, 0))
    const = lambda shape: pl.BlockSpec(shape, lambda b, j, pt: (0,) * len(shape))

    def page(g):
        return pl.BlockSpec((1, BRANCH_WIDTH, PAGE_SIZE),
                            lambda b, j, pt: (pt[b, n_pages - 1 - (j * g_pages + g)], 0, 0))

    return per_b, const, [page(g) for g in range(g_pages)]


def _sb_sample(page_table, qbd, newk, newv, pool_k, pool_v, gmask, n_new):
    db, rows, wd = qbd.shape
    n_pages = page_table.shape[1]
    assert n_pages >= 2
    n_recent = min(SB_RECENT_PAGES, n_pages // 2)
    n_rest = n_pages - n_recent
    g_pages = max(g for g in range(1, MAX_REST_PAGES_PER_STEP + 1) if n_rest % g == 0)
    uext = _reverse_cumsum_matrix(PAGE_SIZE)
    page_block = (1, wd, PAGE_SIZE)

    per_b = lambda shape: pl.BlockSpec(shape, lambda b, pt: (b, 0, 0))
    recent = [pl.BlockSpec(page_block, lambda b, pt, g=g: (pt[b, n_pages - 1 - g], 0, 0))
              for g in range(n_recent)]
    const1 = lambda shape: pl.BlockSpec(shape, lambda b, pt: (0,) * len(shape))
    o_recent, acc, c, done = pl.pallas_call(
        functools.partial(_sb_sample_recent_kernel, n_new=n_new, n_recent=n_recent),
        out_shape=(jax.ShapeDtypeStruct((db, n_new, wd), F32),
                   jax.ShapeDtypeStruct((db, rows, wd), F32),
                   jax.ShapeDtypeStruct((db, rows, LANES), F32),
                   jax.ShapeDtypeStruct((db, SUBLANES, LANES), jnp.int32)),
        grid_spec=pltpu.PrefetchScalarGridSpec(
            num_scalar_prefetch=1,
            grid=(db,),
            in_specs=[per_b((1, rows, wd)), per_b(page_block), per_b(page_block)]
                     + recent + recent + [const1(uext.shape), const1(gmask.shape)],
            out_specs=(per_b((1, n_new, wd)), per_b((1, rows, wd)), per_b((1, rows, LANES)),
                       per_b((1, SUBLANES, LANES))),
        ),
        compiler_params=_cparams(("parallel",)),
        name="sb_sample_recent",
    )(page_table, qbd, newk, newv, *([pool_k] * n_recent), *([pool_v] * n_recent), uext, gmask)
    done = done[:, 0, 0]

    per_b2 = lambda shape: pl.BlockSpec(shape, lambda b, j, pt, dn: (b, 0, 0))
    const2 = lambda shape: pl.BlockSpec(shape, lambda b, j, pt, dn: (0,) * len(shape))

    def older(g):
        def index(b, j, pt, dn):
            page = pt[b, n_rest - 1 - (j * g_pages + g)]
            return (jnp.where(dn[b] == 0, page, 0), 0, 0)
        return pl.BlockSpec(page_block, index)

    older_specs = [older(g) for g in range(g_pages)]
    rest_call = pl.pallas_call(
        functools.partial(_sb_sample_rest_kernel, n_new=n_new, g_pages=g_pages),
        out_shape=jax.ShapeDtypeStruct((db, n_new, wd), F32),
        grid_spec=pltpu.PrefetchScalarGridSpec(
            num_scalar_prefetch=2,
            grid=(db, n_rest // g_pages),
            in_specs=[per_b2((1, rows, wd)), per_b2((1, rows, wd)), per_b2((1, rows, LANES))]
                     + older_specs + older_specs + [const2(uext.shape), const2(gmask.shape)],
            out_specs=per_b2((1, n_new, wd)),
            scratch_shapes=[pltpu.VMEM((rows, wd), F32), pltpu.VMEM((rows, LANES), F32)],
        ),
        compiler_params=_cparams(("parallel", "arbitrary")),
        name="sb_sample_rest",
    )
    return lax.cond(
        jnp.any(done == 0),
        lambda: rest_call(page_table, done, qbd, acc, c, *([pool_k] * g_pages),
                          *([pool_v] * g_pages), uext, gmask),
        lambda: o_recent)


def _diff_sample(page_table, lam_vecs, subln, qbd, newk, newv, pool_k, pool_v, n_new, lam_init):
    db, rows, wd = qbd.shape
    n_pages = page_table.shape[1]
    g_pages = _pages_per_step(n_pages)
    per_b, const, page_specs = _paged_specs(n_pages, g_pages)
    grid_spec = pltpu.PrefetchScalarGridSpec(
        num_scalar_prefetch=1,
        grid=(db, n_pages // g_pages),
        in_specs=[const(lam_vecs.shape), const(subln.shape),
                  per_b((1, rows, wd)), per_b((1, wd, PAGE_SIZE)), per_b((1, wd, PAGE_SIZE))]
                 + page_specs + page_specs,
        out_specs=per_b((1, n_new, wd)),
        scratch_shapes=[pltpu.VMEM((rows, LANES), F32), pltpu.VMEM((rows, LANES), F32),
                        pltpu.VMEM((rows, LANES), F32)],
    )
    return pl.pallas_call(
        functools.partial(_diff_sample_kernel, n_new=n_new, g_pages=g_pages, lam_init=lam_init),
        out_shape=jax.ShapeDtypeStruct((db, n_new, wd), F32),
        grid_spec=grid_spec,
        compiler_params=_cparams(("parallel", "arbitrary")),
        name="diff_sample",
    )(page_table, lam_vecs, subln, qbd, newk, newv, *([pool_k] * g_pages), *([pool_v] * g_pages))


def _merge_kernel(x_ref, osb_ref, odf_ref, omem_ref, g_ref, wg_ref, bg_ref, wb_ref, wo_ref,
                  o_ref):
    x = x_ref[...]
    h = _rms_rows(x, g_ref[...]).astype(BF16)
    merged = None
    for b, ob_ref in enumerate((osb_ref, odf_ref, omem_ref)):
        gate = _dot(h, wg_ref[:, b * D_MODEL:(b + 1) * D_MODEL]) + bg_ref[b:b + 1, :]
        term = jax.nn.sigmoid(gate) * _dot(ob_ref[...].astype(BF16), wb_ref[b])
        merged = term if merged is None else merged + term
    o_ref[...] = x + _dot(merged.astype(BF16), wo_ref[...])


def _merge(x2d, o_sb, o_diff, o_mem, g, w_gate, b_gate, w_branch, w_out, tm):
    m = x2d.shape[0]
    row = lambda i: (i, 0)
    wide = pl.BlockSpec((tm, D_MODEL), row)
    narrow = pl.BlockSpec((tm, BRANCH_WIDTH), row)
    return pl.pallas_call(
        _merge_kernel,
        out_shape=jax.ShapeDtypeStruct((m, D_MODEL), F32),
        grid=(m // tm,),
        in_specs=[wide, narrow, narrow, narrow, _const_spec(g.shape), _const_spec(w_gate.shape),
                  _const_spec(b_gate.shape), _const_spec(w_branch.shape),
                  _const_spec(w_out.shape)],
        out_specs=wide,
        compiler_params=_cparams(("parallel",)),
        name="merge",
    )(x2d, o_sb, o_diff, o_mem, g, w_gate, b_gate, w_branch, w_out)


def _ffn_chunks(d_ff):
    chunk = MXU_WIDTH if d_ff % MXU_WIDTH == 0 else LANES
    return chunk, d_ff // chunk


def _ffn_prompt_kernel(x_ref, g_ref, wup_ref, cw_ref, cb_ref, wdn_ref, o_ref, tail_ref, abuf_ref,
                       *, tm, d_ff):
    pad = SUBLANES

    @pl.when(pl.program_id(1) == 0)
    def _():
        abuf_ref[0:pad, :] = jnp.zeros((pad, d_ff), F32)

    x = x_ref[0]
    h = _rms_rows(x, g_ref[...]).astype(BF16)
    chunk, n_chunks = _ffn_chunks(d_ff)
    acc = x
    for c in range(n_chunks):
        sl = slice(c * chunk, (c + 1) * chunk)
        abuf_ref[pad:pad + tm, sl] = _dot(h, wup_ref[:, sl])
        conv = (cb_ref[:, sl]
                + cw_ref[0:1, sl] * abuf_ref[pad - 2:pad - 2 + tm, sl]
                + cw_ref[1:2, sl] * abuf_ref[pad - 1:pad - 1 + tm, sl]
                + cw_ref[2:3, sl] * abuf_ref[pad:pad + tm, sl])
        up = _dot(h, wup_ref[:, d_ff + c * chunk:d_ff + (c + 1) * chunk])
        acc = acc + _dot((jax.nn.silu(conv) * up).astype(BF16), wdn_ref[sl, :])
    o_ref[0] = acc
    last = abuf_ref[tm:tm + pad, :]
    tail_ref[0] = last
    abuf_ref[0:pad, :] = last


def _ffn_prompt(x3d, g, w_up, conv_w, conv_b, w_down, tm):
    b, t, d = x3d.shape
    d_ff = w_down.shape[0]
    kernel = functools.partial(_ffn_prompt_kernel, tm=tm, d_ff=d_ff)
    xspec = pl.BlockSpec((1, tm, d), lambda bi, ti: (bi, ti, 0))
    return pl.pallas_call(
        kernel,
        out_shape=(jax.ShapeDtypeStruct((b, t, d), F32),
                   jax.ShapeDtypeStruct((b, SUBLANES, d_ff), F32)),
        grid=(b, t // tm),
        in_specs=[xspec, _const_spec(g.shape), _const_spec(w_up.shape), _const_spec(conv_w.shape),
                  _const_spec(conv_b.shape), _const_spec(w_down.shape)],
        out_specs=(xspec, pl.BlockSpec((1, SUBLANES, d_ff), lambda bi, ti: (bi, 0, 0))),
        scratch_shapes=[pltpu.VMEM((tm + SUBLANES, d_ff), F32)],
        compiler_params=_cparams(("parallel", "arbitrary")),
        name="ffn_prompt",
    )(x3d, g, w_up, conv_w, conv_b, w_down)


def _ffn_sample_kernel(x_ref, g_ref, pre_ref, wa_ref, wu_ref, cw_ref, cb_ref, wdn_ref,
                       o_ref, a_ref, h_ref, *, n_new, db):
    c = pl.program_id(0)

    @pl.when(c == 0)
    def _():
        x = x_ref[...]
        h_ref[...] = _rms_rows(x, g_ref[...]).astype(BF16)
        o_ref[...] = x

    h = h_ref[...]
    a = _dot(h, wa_ref[...])
    a_ref[...] = a
    up = _dot(h, wu_ref[...])
    slabs = [pre_ref[0], pre_ref[1]] + [a[t * db:(t + 1) * db] for t in range(n_new)]
    conv = jnp.concatenate(
        [cb_ref[...] + cw_ref[0:1] * slabs[t] + cw_ref[1:2] * slabs[t + 1] + cw_ref[2:3] * slabs[t + 2]
         for t in range(n_new)], axis=0)
    o_ref[...] += _dot((jax.nn.silu(conv) * up).astype(BF16), wdn_ref[...])


def _ffn_sample(x_tm, g, prefix_tm, w_up, conv_w, conv_b, w_down, n_new, db):
    m, d = x_tm.shape
    d_ff = w_down.shape[0]
    chunk, n_chunks = _ffn_chunks(d_ff)
    kernel = functools.partial(_ffn_sample_kernel, n_new=n_new, db=db)
    fixed = lambda shape: pl.BlockSpec(shape, lambda c: (0,) * len(shape))
    return pl.pallas_call(
        kernel,
        out_shape=(jax.ShapeDtypeStruct((m, d), F32), jax.ShapeDtypeStruct((m, d_ff), F32)),
        grid=(n_chunks,),
        in_specs=[fixed((m, d)), fixed(g.shape),
                  pl.BlockSpec((CONV_WIDTH - 1, db, chunk), lambda c: (0, 0, c)),
                  pl.BlockSpec((d, chunk), lambda c: (0, c)),
                  pl.BlockSpec((d, chunk), lambda c: (0, n_chunks + c)),
                  pl.BlockSpec((CONV_WIDTH, chunk), lambda c: (0, c)),
                  pl.BlockSpec((1, chunk), lambda c: (0, c)),
                  pl.BlockSpec((chunk, d), lambda c: (c, 0))],
        out_specs=(fixed((m, d)), pl.BlockSpec((m, chunk), lambda c: (0, c))),
        scratch_shapes=[pltpu.VMEM((m, d), BF16)],
        compiler_params=_cparams(("arbitrary",)),
        name="ffn_sample",
    )(x_tm, g, prefix_tm, w_up, w_up, conv_w, conv_b, w_down)


def _rope_tables(pos):
    half = HEAD64 // 2
    inv_freq = jnp.power(ROPE_THETA, -jnp.arange(half, dtype=F32) / half)
    ang = pos.astype(F32)[:, None] * inv_freq[None, :]
    cos, sin = jnp.cos(ang), jnp.sin(ang)
    cos = jnp.concatenate([cos, cos], axis=-1)
    sin = jnp.concatenate([-sin, sin], axis=-1)
    reps = LANES // HEAD64
    return jnp.tile(cos, (1, reps)), jnp.tile(sin, (1, reps)), cos.T, sin.T


def _seg_matrix(width, seg):
    i = jnp.arange(width) // seg
    return ((i[:, None] == i[None, :]).astype(F32) / seg).astype(BF16)


def _tile_gain(gain, width):
    return jnp.tile(gain.astype(F32), width // gain.shape[0]).reshape(1, width)


def _block_diag_queries(q, t_idx, grp):
    lane_grp = jnp.arange(q.shape[-1]) // HEAD64
    mask = (lane_grp[None, :] == grp[:, None]).astype(q.dtype)
    return jnp.take(q, t_idx, axis=1) * mask[None]


def _state_from_t(yt, lead, heads):
    n = len(heads)
    y = yt.reshape(tuple(heads) + tuple(lead))
    perm = tuple(range(n, n + len(lead))) + tuple(range(n))
    return jnp.transpose(y, perm)


def kernel(x_prompt, x_sample, cache_sb_k, cache_sb_v, cache_diff_k, cache_diff_v, cache_mem_k, cache_mem_v, state_conv, page_table, mem_prompt, norm_mix, norm_mem, w_in, b_gate, diff_q_norm, diff_k_norm, lambda_q1, lambda_k1, lambda_q2, lambda_k2, diff_subln, w_mem_kv, mem_q_norm, mem_k_norm, w_branch, w_out, norm_ffn, w_ffn_up, conv_w, conv_b, w_ffn_down):
    depth = w_in.shape[0]
    assert depth == 1, "single-layer step"
    l = 0
    lam_init = 0.8 - 0.6 * math.exp(-0.3 * l)
    bsz, seq, d = x_prompt.shape
    db, n_new, _ = x_sample.shape
    assert 2 * n_new == SUBLANES, "sample query rows are packed eight (t, map) rows per head"
    n_pages = page_table.shape[1]
    n_pool = cache_sb_k.shape[1]
    past_len = n_pages * PAGE_SIZE
    n_mem = mem_prompt.shape[1]
    wd = BRANCH_WIDTH
    d_ff = w_ffn_down.shape[1]
    n_attn = N_ATTN_SECTIONS * wd
    ms = db * n_new

    w_in_b = w_in[l].astype(BF16)
    sec = lambda s: w_in_b[:, s * wd:(s + 1) * wd]
    w_row = jnp.concatenate([sec(SEC_SB_Q), sec(SEC_D_Q), sec(SEC_D_V), sec(SEC_M_Q)], axis=1)
    w_t = jnp.concatenate([sec(SEC_SB_K), sec(SEC_SB_V), sec(SEC_D_K)], axis=1).T
    w_gate = w_in_b[:, n_attn:]
    w_branch_b = w_branch[l].astype(BF16)
    w_out_b = w_out[l].astype(BF16)
    w_up_b = w_ffn_up[l].astype(BF16)
    w_down_b = w_ffn_down[l].astype(BF16)
    w_kv_b = w_mem_kv[l].astype(BF16)
    g_mix = norm_mix[l].reshape(1, d)
    g_mem = norm_mem[l].reshape(1, d)
    g_ffn = norm_ffn[l].reshape(1, d)
    seg64, seg128 = _seg_matrix(wd, HEAD64), _seg_matrix(wd, HEAD128)
    dqn = _tile_gain(diff_q_norm[l], wd)
    dkn_col = jnp.broadcast_to(_tile_gain(diff_k_norm[l], wd).reshape(wd, 1), (wd, LANES))
    mqn, mkn = _tile_gain(mem_q_norm[l], wd), _tile_gain(mem_k_norm[l], wd)
    subln = diff_subln[l].reshape(1, HEAD128)
    lam_vecs = jnp.stack([lambda_q1[l], lambda_k1[l], lambda_q2[l], lambda_k2[l]]).astype(F32)
    cw, cb = conv_w[l], conv_b[l].reshape(1, d_ff)

    def in_proj(x3d, pos, tm):
        cos_r, sin_r, cos_c, sin_c = _rope_tables(pos)
        return _in_proj(x3d, g_mix, w_row, w_t, cos_r, sin_r, cos_c, sin_c, seg64, seg128,
                        dqn, dkn_col, mqn, tm)

    tm = min(512, seq)
    (sbq_b, dq_b, dv_f, dv_b, mq_b, sbkt_f, sbkt_b, sbvt_f, sbvt_b, dkt_f, dkt_b) = in_proj(
        x_prompt, jnp.arange(seq), tm)
    mk_f, mk_b, mv_f, mv_b = _mem_kv(mem_prompt.reshape(bsz * n_mem, d), g_mem, w_kv_b, seg128, mkn)

    o_sb = _sb_prompt(sbq_b, sbkt_b, sbvt_b)
    o_diff = _diff_prompt(lam_vecs, subln, dq_b, dkt_b, dv_b, lam_init)
    o_mem = _mem_attend_prompt(mq_b, mk_b.reshape(bsz, n_mem, wd), mv_b.reshape(bsz, n_mem, wd), tm)
    x1 = _merge(x_prompt.reshape(bsz * seq, d), o_sb.reshape(-1, wd), o_diff.reshape(-1, wd),
                o_mem.reshape(-1, wd), g_mix, w_gate, b_gate[l], w_branch_b, w_out_b, tm)
    y_prompt, conv_tail = _ffn_prompt(x1.reshape(bsz, seq, d), g_ffn, w_up_b, cw, cb, w_down_b, tm)

    pos_s = jnp.tile(past_len + jnp.arange(n_new), db)
    (ssbq_b, sdq_b, sdv_f, _, smq_b, ssbkt_f, _, ssbvt_f, _, sdkt_f, _) = in_proj(
        x_sample.reshape(1, ms, d), pos_s, ms)

    s3 = lambda a: a.reshape(db, n_new, wd)

    def new_t_page(yt):
        y = yt.reshape(wd, db, n_new).transpose(1, 0, 2)
        return jnp.pad(y, ((0, 0), (0, 0), (0, PAGE_SIZE - n_new)))

    r = jnp.arange(n_new * SUBLANES)
    t_sb, grp_sb = r // SUBLANES, r % SUBLANES
    t_df, grp_df = (r % SUBLANES) // 2, 2 * (r // SUBLANES) + r % 2
    gmask_sb = ((jnp.arange(wd) // HEAD64)[None, :] == grp_sb[:, None]).astype(F32)

    t_pages = lambda c: jnp.swapaxes(c[l].reshape(n_pool, PAGE_SIZE, wd), 1, 2)
    row_pages = lambda c: c[l].reshape(c.shape[1], -1, HEAD128)

    so_sb = _sb_sample(page_table, _block_diag_queries(s3(ssbq_b), t_sb, grp_sb),
                       new_t_page(ssbkt_f), new_t_page(ssbvt_f),
                       t_pages(cache_sb_k), t_pages(cache_sb_v), gmask_sb, n_new)
    new_dv = jnp.pad(sdv_f.reshape(db, n_new * DIFF_HEADS, HEAD128),
                     ((0, 0), (0, (PAGE_SIZE - n_new) * DIFF_HEADS), (0, 0)))
    so_diff = _diff_sample(page_table, lam_vecs, subln, _block_diag_queries(s3(sdq_b), t_df, grp_df),
                           new_t_page(sdkt_f), new_dv,
                           t_pages(cache_diff_k), row_pages(cache_diff_v), n_new, lam_init)
    q_rows = 2 * SUBLANES
    q_pad = jnp.pad(s3(smq_b), ((0, 0), (0, q_rows - n_new), (0, 0)))
    so_mem = _mem_attend_sample(q_pad, row_pages(cache_mem_k), row_pages(cache_mem_v), n_mem)[:, :n_new]
    xs1 = _merge(x_sample.reshape(ms, d), so_sb.reshape(ms, wd), so_diff.reshape(ms, wd),
                 so_mem.reshape(ms, wd), g_mix, w_gate, b_gate[l], w_branch_b, w_out_b, ms)
    xs1_tm = xs1.reshape(db, n_new, d).transpose(1, 0, 2).reshape(ms, d)
    prefix_tm = state_conv[l].transpose(1, 0, 2)
    ys_tm, a_tm = _ffn_sample(xs1_tm, g_ffn, prefix_tm, w_up_b, cw, cb, w_down_b, n_new, db)
    y_sample = ys_tm.reshape(n_new, db, d).transpose(1, 0, 2)
    a_ext = jnp.concatenate([prefix_tm, a_tm.reshape(n_new, db, d_ff)], axis=0)
    conv_s = a_ext[n_new:].transpose(1, 0, 2)

    st = lambda a, *shape: a.reshape((1,) + shape)
    sb_heads, df_heads = (SB_HEADS, HEAD64), (DIFF_HEADS, 2, HEAD64)
    return (
        y_prompt, y_sample,
        jnp.transpose(sbkt_f.reshape(bsz, SB_HEADS, HEAD64, seq), (0, 3, 1, 2))[None],
        jnp.transpose(sbvt_f.reshape(bsz, SB_HEADS, HEAD64, seq), (0, 3, 1, 2))[None],
        jnp.transpose(dkt_f.reshape(bsz, DIFF_HEADS, 2, HEAD64, seq), (0, 4, 1, 2, 3))[None],
        st(dv_f, bsz, seq, DIFF_HEADS, HEAD128),
        st(mk_f, bsz, n_mem, MEM_HEADS, HEAD128), st(mv_f, bsz, n_mem, MEM_HEADS, HEAD128),
        st(conv_tail[:, SUBLANES - (CONV_WIDTH - 1):], bsz, CONV_WIDTH - 1, d_ff),
        _state_from_t(ssbkt_f, (db, n_new), sb_heads)[None],
        _state_from_t(ssbvt_f, (db, n_new), sb_heads)[None],
        _state_from_t(sdkt_f, (db, n_new), df_heads)[None],
        st(sdv_f, db, n_new, DIFF_HEADS, HEAD128),
        st(conv_s, db, CONV_WIDTH - 1, d_ff),
    )
```
